```python
import jax, jax.numpy as jnp
from jax import lax
import numpy as np

D_MODEL = 2048
BATCH = 16
SEQ = 2048
DEPTH = 4

N_MIXERS = 3
RMS_EPS = 1e-6

RWKV_HEAD_DIM = 64
RWKV_HEADS = D_MODEL // RWKV_HEAD_DIM
DECAY_LORA = 96
AAA_LORA = 96
MV_LORA = 64
RWKV_LN_EPS = 64e-5

ATT_HEAD_DIM = 128
DILATED_CONFIG = ((128, 1), (512, 4), (2048, 16))
ATT_GROUPS = len(DILATED_CONFIG)
ATT_HEADS_PER_GROUP = D_MODEL // (2 * ATT_HEAD_DIM)
ATT_BRANCH = ATT_HEADS_PER_GROUP * ATT_HEAD_DIM
ATT_QKV = 3 * ATT_GROUPS * ATT_BRANCH
ATT_BLOCK = 128
ROPE_THETA = 500000.0
ROPE_DIM = ATT_HEAD_DIM // 4

POOL_WINDOWS = (2, 4, 8, 16)
POOL_WIDTH = D_MODEL
POOL_GROUP = POOL_WIDTH // len(POOL_WINDOWS)

kernel_name = "hybrid_rwkv7_dilated_attn_pool_trunk"


def _rms_norm(x, w):
    xf = x.astype(jnp.float32)
    y = xf * lax.rsqrt(jnp.mean(xf * xf, axis=-1, keepdims=True) + RMS_EPS)
    return (y * w.astype(jnp.float32)).astype(x.dtype)


def _rwkv7_scan(r, w, k, v, a, b):
    B, S, H, N = r.shape

    def step(state, inp):
        r_t, w_t, k_t, v_t, a_t, b_t = inp
        sa = jnp.einsum('bhij,bhj->bhi', state, a_t)
        state = (state * w_t[:, :, None, :] + sa[..., None] * b_t[:, :, None, :]
                 + v_t[..., None] * k_t[:, :, None, :])
        y_t = jnp.einsum('bhij,bhj->bhi', state, r_t)
        return state, y_t

    xs = tuple(jnp.moveaxis(t, 1, 0) for t in (r, w, k, v, a, b))
    init = jnp.zeros((B, H, N, N), jnp.float32)
    _, ys = lax.scan(step, init, xs)
    return jnp.moveaxis(ys, 0, 1)


def _head_group_norm(y, w, b):
    B, S, H, N = y.shape
    mu = jnp.mean(y, axis=-1, keepdims=True)
    yc = y - mu
    var = jnp.mean(yc * yc, axis=-1, keepdims=True)
    yn = (yc * lax.rsqrt(var + RWKV_LN_EPS)).reshape(B, S, H * N)
    return yn * w.astype(jnp.float32) + b.astype(jnp.float32)


def _rwkv7_mixer(h, mu, w_in, w0, w1, w2, a0, a1, a2, k_k, k_a, r_k, lnx_w, lnx_b,
                 w_out, v_first, vres):
    B, S, D = h.shape
    H, N = RWKV_HEADS, RWKV_HEAD_DIM
    heads = lambda t: t.reshape(B, S, H, N)
    xx = jnp.pad(h, ((0, 0), (1, 0), (0, 0)))[:, :S] - h
    r, k, v, g = [(h + xx * mu[j]) @ w_in[j] for j in range(4)]
    xw = h + xx * mu[4]
    xa = h + xx * mu[5]
    w_log = -jax.nn.softplus(-(w0 + jnp.tanh(xw @ w1) @ w2).astype(jnp.float32)) - 0.5
    decay = jnp.exp(-jnp.exp(w_log))
    if vres is None:
        v_first = v
    else:
        v0, v1, v2 = vres
        xv = h + xx * mu[2]
        v = v + (v_first - v) * jax.nn.sigmoid(v0 + (xv @ v1) @ v2)
    a = jax.nn.sigmoid(a0 + (xa @ a1) @ a2)
    kk = heads(k * k_k).astype(jnp.float32)
    kk = kk / jnp.maximum(jnp.sqrt(jnp.sum(kk * kk, axis=-1, keepdims=True)), 1e-12)
    k = k * (1 + (a - 1) * k_a)
    af = heads(a).astype(jnp.float32)
    vf = heads(v).astype(jnp.float32)
    y = _rwkv7_scan(heads(r).astype(jnp.float32), heads(decay), heads(k).astype(jnp.float32),
                    vf, -kk, kk * af)
    y = _head_group_norm(y, lnx_w, lnx_b)
    bonus = jnp.sum(heads(r * k).astype(jnp.float32) * r_k.astype(jnp.float32),
                    axis=-1, keepdims=True) * vf
    y = (y + bonus.reshape(B, S, D)).astype(h.dtype) * jax.nn.silu(g)
    return y @ w_out, v_first


def _rope_tables(positions):
    inv_freq = ROPE_THETA ** (-jnp.arange(0, ROPE_DIM, 2, dtype=jnp.float32) / ROPE_DIM)
    ang = positions.astype(jnp.float32)[..., None] * inv_freq
    return jnp.cos(ang), jnp.sin(ang)


def _apply_partial_rope(x, cos, sin):
    half = ROPE_DIM // 2
    c = cos[:, :, None, None, :].astype(x.dtype)
    s = sin[:, :, None, None, :].astype(x.dtype)
    x1 = x[..., :half]
    x2 = x[..., half:ROPE_DIM]
    return jnp.concatenate([x1 * c - x2 * s, x1 * s + x2 * c, x[..., ROPE_DIM:]], axis=-1)


def _dilated_window_attention(q, k, v, window, dilation):
    B, S, H, Dh = q.shape
    steps = window // dilation
    L = S // dilation
    nblk = -(-L // ATT_BLOCK)
    pad = nblk * ATT_BLOCK - L

    def gather(t):
        t = t.reshape(B, L, dilation, H, Dh).transpose(0, 2, 3, 1, 4)
        t = jnp.pad(t, ((0, 0), (0, 0), (0, 0), (0, pad), (0, 0)))
        return t.reshape(B, dilation, H, nblk, ATT_BLOCK, Dh)

    def with_prev(t):
        prev = jnp.pad(t[:, :, :, :-1], ((0, 0), (0, 0), (0, 0), (1, 0), (0, 0), (0, 0)))
        return jnp.concatenate([prev, t], axis=4)

    qb = gather(q)
    kc = with_prev(gather(k))
    vc = with_prev(gather(v))
    qi = jnp.arange(ATT_BLOCK)[:, None]
    kj = jnp.arange(2 * ATT_BLOCK)[None, :]
    dist = ATT_BLOCK + qi - kj
    key_idx = (jnp.arange(nblk)[:, None, None] - 1) * ATT_BLOCK + kj[None]
    mask = (dist >= 0)[None] & (dist <= steps)[None] & (key_idx >= 0)
    s = jnp.einsum('bghnqe,bghnke->bghnqk', qb, kc).astype(jnp.float32) * (Dh ** -0.5)
    s = jnp.where(mask, s, -jnp.inf)
    m = jnp.max(s, axis=-1, keepdims=True)
    p = jnp.exp(s - m)
    l = jnp.sum(p, axis=-1, keepdims=True)
    o = jnp.einsum('bghnqk,bghnke->bghnqe', p, vc.astype(jnp.float32)) / l
    lse = (m + jnp.log(l))[..., 0]
    o = o.reshape(B, dilation, H, nblk * ATT_BLOCK, Dh)[:, :, :, :L]
    o = o.transpose(0, 3, 1, 2, 4).reshape(B, S, H, Dh)
    lse = lse.reshape(B, dilation, H, nblk * ATT_BLOCK)[:, :, :, :L]
    lse = lse.transpose(0, 3, 1, 2).reshape(B, S, H)
    return o, lse


def _dilated_attn_mixer(h, positions, w_in, qn_w, kn_w, w_out):
    B, S, _ = h.shape
    proj = h @ w_in
    qkv = proj[..., :ATT_QKV].reshape(B, S, 3, ATT_GROUPS, ATT_HEADS_PER_GROUP, ATT_HEAD_DIM)
    gate = proj[..., ATT_QKV:]
    cos, sin = _rope_tables(positions)
    q = _apply_partial_rope(_rms_norm(qkv[:, :, 0], qn_w[:, None, :]), cos, sin)
    k = _apply_partial_rope(_rms_norm(qkv[:, :, 1], kn_w[:, None, :]), cos, sin)
    v = qkv[:, :, 2]
    outs, lses = [], []
    for gi, (window, dilation) in enumerate(DILATED_CONFIG):
        o_g, lse_g = _dilated_window_attention(q[:, :, gi], k[:, :, gi], v[:, :, gi], window, dilation)
        outs.append(o_g)
        lses.append(lse_g)
    wts = jax.nn.softmax(jnp.stack(lses, axis=0), axis=0)
    o = jnp.sum(wts[..., None] * jnp.stack(outs, axis=0), axis=0)
    o = o.astype(h.dtype).reshape(B, S, ATT_BRANCH)
    return (o * jax.nn.silu(gate)) @ w_out


def _pool_mixer(h, w_in, w_grp, scale, w_out):
    B, S, _ = h.shape
    u, gate = jnp.split(h @ w_in, 2, axis=-1)
    c = jnp.pad(jnp.cumsum(u.astype(jnp.float32), axis=1), ((0, 0), (1, 0), (0, 0)))
    t = jnp.arange(S)
    pooled = []
    for j, win in enumerate(POOL_WINDOWS):
        cg = c[..., j * POOL_GROUP:(j + 1) * POOL_GROUP]
        lower = jnp.pad(cg[:, :S + 1 - win], ((0, 0), (win - 1, 0), (0, 0)))
        cnt = jnp.minimum(t + 1, win).astype(jnp.float32)[None, :, None]
        pooled.append((cg[:, 1:] - lower) / cnt)
    pooled = jnp.stack(pooled, axis=2).astype(h.dtype)
    diff = pooled - u.reshape(B, S, len(POOL_WINDOWS), POOL_GROUP)
    y = jnp.einsum('bsgc,gce->bsge', diff, w_grp).reshape(B, S, POOL_WIDTH) * scale
    return (y * jax.nn.silu(gate)) @ w_out


def setup_inputs(seed: int = 0) -> dict:
    key = jax.random.key(seed)
    ks = iter(jax.random.split(key, 40))
    f32 = jnp.float32
    n_a = (DEPTH + 2) // 3
    n_b = (DEPTH + 1) // 3
    n_c = DEPTH // 3
    D, H, N = D_MODEL, RWKV_HEADS, RWKV_HEAD_DIM

    def nrm(shape, scale):
        return jax.random.normal(next(ks), shape, f32) * scale

    x = jax.random.normal(next(ks), (BATCH, SEQ, D), f32)
    offset = jax.random.randint(next(ks), (BATCH, 1), 0, 4096, dtype=jnp.int32)
    positions = offset + jnp.arange(SEQ, dtype=jnp.int32)[None, :]
    norm_w = 1.0 + nrm((DEPTH, D), 0.1)
    lin = jnp.linspace(0.0, 1.0, D, dtype=f32)
    inp = {
        'x': x,
        'positions': positions,
        'norm_w': norm_w,
        'a_mu': jax.random.uniform(next(ks), (n_a, 6, D), f32),
        'a_w_in': nrm((n_a, 4, D, D), D ** -0.5),
        'a_w0': (-6.5 + 5.0 * lin ** 1.5)[None, :] + nrm((n_a, D), 0.1),
        'a_w1': nrm((n_a, D, DECAY_LORA), D ** -0.5),
        'a_w2': nrm((n_a, DECAY_LORA, D), 0.1 * DECAY_LORA ** -0.5),
        'a_a0': nrm((n_a, D), 0.3),
        'a_a1': nrm((n_a, D, AAA_LORA), D ** -0.5),
        'a_a2': nrm((n_a, AAA_LORA, D), 0.3 * AAA_LORA ** -0.5),
        'a_k_k': 0.85 + nrm((n_a, D), 0.05),
        'a_k_a': 1.0 + nrm((n_a, D), 0.05),
        'a_r_k': nrm((n_a, H, N), 0.1),
        'a_lnx_w': 1.0 + nrm((n_a, D), 0.1),
        'a_lnx_b': nrm((n_a, D), 0.02),
        'a_v0': 1.0 + nrm((n_a - 1, D), 0.1),
        'a_v1': nrm((n_a - 1, D, MV_LORA), D ** -0.5),
        'a_v2': nrm((n_a - 1, MV_LORA, D), 0.3 * MV_LORA ** -0.5),
        'a_w_out': nrm((n_a, D, D), 0.5 * D ** -0.5),
        'b_w_in': nrm((n_b, D, ATT_QKV + ATT_BRANCH), D ** -0.5),
        'b_qn_w': 1.0 + nrm((n_b, ATT_GROUPS, ATT_HEAD_DIM), 0.1),
        'b_kn_w': 1.0 + nrm((n_b, ATT_GROUPS, ATT_HEAD_DIM), 0.1),
        'b_w_out': nrm((n_b, ATT_BRANCH, D), 0.5 * ATT_BRANCH ** -0.5),
        'c_w_in': nrm((n_c, D, 2 * POOL_WIDTH), D ** -0.5),
        'c_w_grp': nrm((n_c, len(POOL_WINDOWS), POOL_GROUP, POOL_GROUP), POOL_GROUP ** -0.5),
        'c_scale': 1.0 + nrm((n_c, POOL_WIDTH), 0.1),
        'c_w_out': nrm((n_c, POOL_WIDTH, D), 0.5 * POOL_WIDTH ** -0.5),
    }
    return inp


def reference(x, positions, norm_w, a_mu, a_w_in, a_w0, a_w1, a_w2, a_a0, a_a1, a_a2,
              a_k_k, a_k_a, a_r_k, a_lnx_w, a_lnx_b, a_v0, a_v1, a_v2, a_w_out,
              b_w_in, b_qn_w, b_kn_w, b_w_out, c_w_in, c_w_grp, c_scale, c_w_out):
    ia = ib = ic = 0
    v_first = None
    for i in range(DEPTH):
        h = _rms_norm(x, norm_w[i])
        kind = i % N_MIXERS
        if kind == 0:
            vres = None if ia == 0 else (a_v0[ia - 1], a_v1[ia - 1], a_v2[ia - 1])
            y, vf = _rwkv7_mixer(h, a_mu[ia], a_w_in[ia], a_w0[ia], a_w1[ia], a_w2[ia],
                                 a_a0[ia], a_a1[ia], a_a2[ia], a_k_k[ia], a_k_a[ia], a_r_k[ia],
                                 a_lnx_w[ia], a_lnx_b[ia], a_w_out[ia], v_first, vres)
            if v_first is None:
                v_first = vf
            ia += 1
        elif kind == 1:
            y = _dilated_attn_mixer(h, positions, b_w_in[ib], b_qn_w[ib], b_kn_w[ib], b_w_out[ib])
            ib += 1
        else:
            y = _pool_mixer(h, c_w_in[ic], c_w_grp[ic], c_scale[ic], c_w_out[ic])
            ic += 1
        x = x + y
    return x
```

```python
import functools

import jax
import jax.numpy as jnp
from jax import lax
from jax.experimental import pallas as pl
from jax.experimental.pallas import tpu as pltpu

F32 = jnp.float32
BF16 = jnp.bfloat16

RMS_EPS = 1e-6
RWKV_HEAD_DIM = 64
RWKV_LN_EPS = 64e-5
ATT_HEAD_DIM = 128
ATT_HEADS = 8
ATT_BRANCH = ATT_HEADS * ATT_HEAD_DIM
DILATED_CONFIG = ((128, 1), (512, 4), (2048, 16))
ATT_BLOCK = 128
ROPE_THETA = 500000.0
ROPE_DIM = ATT_HEAD_DIM // 4
POOL_WINDOWS = (2, 4, 8, 16)
POOL_HALO = 16

V7X_LANES = 128
V7X_MXU_DIM = 256
V7X_VMEM_BYTES = 64 * 1024 * 1024
VMEM_LIMIT = 48 * 1024 * 1024

SCAN_CHUNK = 64
SCAN_LANES = V7X_MXU_DIM
SCAN_HEADS = SCAN_LANES // RWKV_HEAD_DIM
SCAN_DOUBLINGS = 5


def _params(sem):
    return pltpu.CompilerParams(dimension_semantics=sem, vmem_limit_bytes=VMEM_LIMIT)


def _tile(n, pref):
    t = min(n, pref)
    while n % t:
        t //= 2
    return t


def _mm_body(a_ref, w_ref, o_ref):
    o_ref[...] = jnp.dot(a_ref[...], w_ref[...], preferred_element_type=F32).astype(o_ref.dtype)


def _mm_res_body(a_ref, w_ref, r_ref, o_ref):
    acc = jnp.dot(a_ref[...], w_ref[...], preferred_element_type=F32)
    o_ref[...] = (r_ref[...] + acc).astype(o_ref.dtype)


def _matmul(a, w, res=None, out_dtype=F32, tm=1024, tn=1024):
    G, M, K = a.shape
    N = w.shape[-1]
    tm = _tile(M, tm)
    tn = _tile(N, tn)
    in_specs = [pl.BlockSpec((None, tm, K), lambda g, i, j: (g, i, 0)),
                pl.BlockSpec((None, K, tn), lambda g, i, j: (g, 0, j))]
    args = [a, w]
    body = _mm_body
    if res is not None:
        in_specs.append(pl.BlockSpec((None, tm, tn), lambda g, i, j: (g, i, j)))
        args.append(res)
        body = _mm_res_body
    return pl.pallas_call(
        body,
        grid=(G, M // tm, N // tn),
        in_specs=in_specs,
        out_specs=pl.BlockSpec((None, tm, tn), lambda g, i, j: (g, i, j)),
        out_shape=jax.ShapeDtypeStruct((G, M, N), out_dtype),
        compiler_params=_params(("parallel", "parallel", "arbitrary")),
        name="matmul",
    )(*args)


def _rms(x, w):
    return (x * lax.rsqrt(jnp.mean(x * x, axis=-1, keepdims=True) + RMS_EPS)) * w


def _norm_body(x_ref, nw_ref, o_ref):
    o_ref[...] = _rms(x_ref[...], nw_ref[...]).astype(o_ref.dtype)


def _norm(x2d, nw, ts=512):
    M, D = x2d.shape
    ts = _tile(M, ts)
    return pl.pallas_call(
        _norm_body,
        grid=(M // ts,),
        in_specs=[pl.BlockSpec((ts, D), lambda i: (i, 0)),
                  pl.BlockSpec((1, D), lambda i: (0, 0))],
        out_specs=pl.BlockSpec((ts, D), lambda i: (i, 0)),
        out_shape=jax.ShapeDtypeStruct((M, D), BF16),
        compiler_params=_params(("parallel",)),
        name="rmsnorm",
    )(x2d, nw.reshape(1, D))


def _rwkv_prep_body(x_ref, xp_ref, nw_ref, mu_ref, o_ref):
    s = pl.program_id(1)
    nw = nw_ref[...]
    h = _rms(x_ref[...], nw)
    hp = _rms(xp_ref[...], nw)[7:8, :]
    hp = jnp.where(s == 0, 0.0, hp)
    row = lax.broadcasted_iota(jnp.int32, h.shape, 0)
    hs = jnp.where(row == 0, hp, pltpu.roll(h, 1, axis=0))
    xx = hs - h
    for j in range(6):
        o_ref[j] = (h + xx * mu_ref[j:j + 1, :]).astype(o_ref.dtype)


def _rwkv_prep(x, nw, mu, ts=256):
    B, S, D = x.shape
    ts = _tile(S, ts)
    sub = 8
    return pl.pallas_call(
        _rwkv_prep_body,
        grid=(B, S // ts),
        in_specs=[pl.BlockSpec((None, ts, D), lambda b, s: (b, s, 0)),
                  pl.BlockSpec((None, sub, D),
                               lambda b, s: (b, jnp.maximum(s * (ts // sub) - 1, 0), 0)),
                  pl.BlockSpec((1, D), lambda b, s: (0, 0)),
                  pl.BlockSpec((6, D), lambda b, s: (0, 0))],
        out_specs=pl.BlockSpec((6, None, ts, D), lambda b, s: (0, b, s, 0)),
        out_shape=jax.ShapeDtypeStruct((6, B, S, D), BF16),
        compiler_params=_params(("parallel", "arbitrary")),
        name="rwkv_prep",
    )(x, x, nw.reshape(1, D), mu)


def _rwkv_lora_body(*refs, has_v):
    if has_v:
        (xw_ref, xa_ref, xv_ref, v_ref, vf_ref,
         w1_ref, w2_ref, w0_ref, a1_ref, a2_ref, a0_ref, v1_ref, v2_ref, v0_ref,
         olw_ref, oa_ref, ov_ref) = refs
    else:
        (xw_ref, xa_ref, w1_ref, w2_ref, w0_ref, a1_ref, a2_ref, a0_ref,
         olw_ref, oa_ref) = refs
    t = jnp.tanh(jnp.dot(xw_ref[...], w1_ref[...], preferred_element_type=F32))
    z = w0_ref[...] + jnp.dot(t.astype(BF16), w2_ref[...], preferred_element_type=F32)
    w_log = -jax.nn.softplus(-z) - 0.5
    olw_ref[...] = -jnp.exp(w_log)
    ta = jnp.dot(xa_ref[...], a1_ref[...], preferred_element_type=F32)
    oa_ref[...] = jax.nn.sigmoid(
        a0_ref[...] + jnp.dot(ta.astype(BF16), a2_ref[...], preferred_element_type=F32))
    if has_v:
        tv = jnp.dot(xv_ref[...], v1_ref[...], preferred_element_type=F32)
        sv = jax.nn.sigmoid(
            v0_ref[...] + jnp.dot(tv.astype(BF16), v2_ref[...], preferred_element_type=F32))
        v = v_ref[...]
        ov_ref[...] = v + (vf_ref[...] - v) * sv


def _pad_lora(w1, w2):
    r = w1.shape[1]
    rp = -(-r // V7X_LANES) * V7X_LANES
    return (jnp.pad(w1, ((0, 0), (0, rp - r))).astype(BF16),
            jnp.pad(w2, ((0, rp - r), (0, 0))).astype(BF16))


def _rwkv_lora(mix, proj, v_first, w0, w1, w2, a0, a1, a2, vres, tm=256):
    _, M, D = mix.shape
    tm = _tile(M, tm)
    has_v = vres is not None
    row = lambda j: pl.BlockSpec((None, tm, D), lambda i, j=j: (j, i, 0))
    full = lambda a: pl.BlockSpec(a.shape, lambda i: (0,) * a.ndim)
    tile = pl.BlockSpec((tm, D), lambda i: (i, 0))
    w1p, w2p = _pad_lora(w1, w2)
    a1p, a2p = _pad_lora(a1, a2)
    weights = [w1p, w2p, w0.reshape(1, D), a1p, a2p, a0.reshape(1, D)]
    args = [mix, mix]
    in_specs = [row(4), row(5)]
    n_out = 2
    if has_v:
        v0, v1, v2 = vres
        v1p, v2p = _pad_lora(v1, v2)
        weights += [v1p, v2p, v0.reshape(1, D)]
        args += [mix, proj, v_first]
        in_specs += [row(2), row(2), tile]
        n_out = 3
    args += weights
    in_specs += [full(w) for w in weights]
    return pl.pallas_call(
        functools.partial(_rwkv_lora_body, has_v=has_v),
        grid=(M // tm,),
        in_specs=in_specs,
        out_specs=[tile] * n_out,
        out_shape=[jax.ShapeDtypeStruct((M, D), F32)] * n_out,
        compiler_params=_params(("parallel",)),
        name="rwkv_lora",
    )(*args)


def _split3(x):
    hi = x.astype(BF16)
    r1 = x - hi.astype(F32)
    mid = r1.astype(BF16)
    lo = (r1 - mid.astype(F32)).astype(BF16)
    return hi, mid, lo


def _dot_t(a, b):
    return lax.dot_general(a, b, (((1,), (1,)), ((), ())), preferred_element_type=F32)


def _dot_tl(a, b):
    return lax.dot_general(a, b, (((0,), (0,)), ((), ())), preferred_element_type=F32)


def _scan_body(r_ref, k_ref, v_ref, lw_ref, a_ref, g_ref,
               kk_ref, ka_ref, rk_ref, lnw_ref, lnb_ref, o_ref, state_ref, *, n_chunks):
    C, L, H, N = SCAN_CHUNK, SCAN_LANES, SCAN_HEADS, RWKV_HEAD_DIM

    @pl.when(pl.program_id(2) == 0)
    def _():
        state_ref[...] = jnp.zeros_like(state_ref)

    lane_head = lax.broadcasted_iota(jnp.int32, (C, L), 1) // N
    head_masks = [lane_head == h for h in range(H)]
    t_idx = lax.broadcasted_iota(jnp.int32, (C, H * C), 0)
    s_idx = lax.broadcasted_iota(jnp.int32, (C, H * C), 1) % C
    strict_lower = s_idx < t_idx
    lower = s_idx <= t_idx
    eye = (s_idx == t_idx).astype(F32)
    bd_mask = (lax.broadcasted_iota(jnp.int32, (L, L), 0) // N
               == lax.broadcasted_iota(jnp.int32, (L, L), 1) // N)
    ones_bd = bd_mask.astype(BF16)
    tri = (lax.broadcasted_iota(jnp.int32, (C, C), 1)
           <= lax.broadcasted_iota(jnp.int32, (C, C), 0)).astype(BF16)

    def blockstack(x):
        return jnp.concatenate([jnp.where(m, x, 0.0) for m in head_masks], axis=0).astype(BF16)

    def segsum(x):
        parts = jnp.concatenate(_split3(x), axis=0)
        s = jnp.dot(parts, ones_bd, preferred_element_type=F32)
        return s[:C] + s[C:2 * C] + s[2 * C:]

    k_k = kk_ref[...]
    k_a = ka_ref[...]
    r_k = rk_ref[...]
    ln_w = lnw_ref[...]
    ln_b = lnb_ref[...]

    def chunk(ci, carry):
        sl = pl.ds(pl.multiple_of(ci * C, C), C)
        r = r_ref[sl, :]
        k = k_ref[sl, :]
        v = v_ref[sl, :]
        lw = lw_ref[sl, :]
        a = a_ref[sl, :]
        g = g_ref[sl, :]

        kk = k * k_k
        kk = kk / jnp.maximum(jnp.sqrt(segsum(kk * kk)), 1e-12)
        k2 = k * (1.0 + (a - 1.0) * k_a)
        b_s = kk * a

        cs = jnp.dot(tri, jnp.concatenate(_split3(lw), axis=1), preferred_element_type=F32)
        cum = cs[:, :L] + cs[:, L:2 * L] + cs[:, 2 * L:]
        total = cum[C - 1:C, :]
        w_in = jnp.exp(cum)
        w_inv = jnp.exp(-cum)
        w_tail = jnp.exp(total - cum)
        at = -kk * jnp.exp(cum - lw)
        rt = r * w_in
        bt = b_s * w_inv
        kt = k2 * w_inv

        lhs = jnp.concatenate([at, rt], axis=0).astype(BF16)
        rhs_t = jnp.concatenate([blockstack(bt), blockstack(kt)], axis=0)
        aa = _dot_t(lhs, rhs_t)
        hc = H * C
        a_ab = jnp.where(strict_lower, aa[:C, :hc], 0.0)
        a_ak = jnp.where(strict_lower, aa[:C, hc:], 0.0)
        a_rb = jnp.where(lower, aa[C:, :hc], 0.0)
        a_rk = jnp.where(lower, aa[C:, hc:], 0.0)

        p = a_ab
        t_inv = eye + a_ab
        p = jnp.dot(p.astype(BF16), blockstack(p), preferred_element_type=F32)
        for i in range(SCAN_DOUBLINGS):
            bd = blockstack(p)
            if i + 1 < SCAN_DOUBLINGS:
                pt = jnp.dot(jnp.concatenate([p, t_inv], axis=0).astype(BF16), bd,
                             preferred_element_type=F32)
                p = pt[:C]
                t_inv = t_inv + pt[C:]
            else:
                t_inv = t_inv + jnp.dot(t_inv.astype(BF16), bd, preferred_element_type=F32)

        state = state_ref[...]
        st = state.astype(BF16)
        bs_v = blockstack(v)
        z = _dot_t(at.astype(BF16), st) + jnp.dot(a_ak.astype(BF16), bs_v,
                                                  preferred_element_type=F32)
        u = jnp.dot(t_inv.astype(BF16), blockstack(z), preferred_element_type=F32)
        y = (_dot_t(rt.astype(BF16), st)
             + jnp.dot(jnp.concatenate([a_rb, a_rk], axis=1).astype(BF16),
                       jnp.concatenate([blockstack(u), bs_v], axis=0),
                       preferred_element_type=F32))
        upd = _dot_tl(jnp.concatenate([u, v], axis=0).astype(BF16),
                      jnp.concatenate([b_s * w_tail, k2 * w_tail], axis=0).astype(BF16))
        state_ref[...] = state * jnp.exp(total) + jnp.where(bd_mask, upd, 0.0)

        inv_n = 1.0 / N
        yc = y - segsum(y) * inv_n
        var = segsum(yc * yc) * inv_n
        yn = (yc * lax.rsqrt(var + RWKV_LN_EPS)) * ln_w + ln_b
        bonus = segsum(r * k2 * r_k) * v
        o_ref[sl, :] = ((yn + bonus) * (g * jax.nn.sigmoid(g))).astype(o_ref.dtype)
        return carry

    lax.fori_loop(0, n_chunks, chunk, 0)


def _rwkv_scan(proj, v, lw, a, k_k, k_a, r_k, ln_w, ln_b, B, S, ts=256):
    _, M, D = proj.shape
    ts = _tile(S, ts)
    L = SCAN_LANES
    nt = S // ts
    pj = lambda j: pl.BlockSpec((None, ts, L), lambda b, h, c, j=j: (j, b * nt + c, h))
    tile = pl.BlockSpec((ts, L), lambda b, h, c: (b * nt + c, h))
    par = pl.BlockSpec((1, L), lambda b, h, c: (0, h))
    vec = lambda p: p.reshape(1, D)
    return pl.pallas_call(
        functools.partial(_scan_body, n_chunks=ts // SCAN_CHUNK),
        grid=(B, D // L, nt),
        in_specs=[pj(0), pj(1), tile, tile, tile, pj(3), par, par, par, par, par],
        out_specs=tile,
        out_shape=jax.ShapeDtypeStruct((M, D), BF16),
        scratch_shapes=[pltpu.VMEM((L, L), F32)],
        compiler_params=_params(("parallel", "parallel", "arbitrary")),
        name="rwkv_scan",
    )(proj, proj, v, lw, a, proj, vec(k_k), vec(k_a), vec(r_k), vec(ln_w), vec(ln_b))


def _rwkv_layer(x, nw, mu, w_in, w0, w1, w2, a0, a1, a2, k_k, k_a, r_k, ln_w, ln_b, w_out,
                v_first, vres):
    B, S, D = x.shape
    M = B * S
    mix = _rwkv_prep(x, nw, mu).reshape(6, M, D)
    proj = _matmul(mix[:4], w_in.astype(BF16))
    outs = _rwkv_lora(mix, proj, v_first, w0, w1, w2, a0, a1, a2, vres)
    if vres is None:
        lw, a = outs
        v = proj[2]
        v_first = v
    else:
        lw, a, v = outs
    yg = _rwkv_scan(proj, v, lw, a, k_k, k_a, r_k, ln_w, ln_b, B, S)
    x2 = _matmul(yg[None], w_out.astype(BF16)[None], res=x.reshape(1, M, D))
    return x2.reshape(B, S, D), v_first


def _qkv_prep_body(pos_ref, invf_ref, q_ref, k_ref, v_ref, qn_ref, kn_ref,
                   oq_ref, ok_ref, ov_ref):
    Dh = ATT_HEAD_DIM
    half = ROPE_DIM // 2
    ang = pos_ref[...].astype(F32) * invf_ref[...]
    cos = jnp.cos(ang)
    sin = jnp.sin(ang)
    lane = lax.broadcasted_iota(jnp.int32, ang.shape, 1)
    s_lo = jnp.where(lane < half, -sin, 0.0)
    s_hi = jnp.where((lane >= half) & (lane < ROPE_DIM), sin, 0.0)

    def norm_rope(x, w):
        xn = _rms(x, w)
        return (xn * cos + pltpu.roll(xn, Dh - half, axis=1) * s_lo
                + pltpu.roll(xn, half, axis=1) * s_hi)

    qn = qn_ref[...]
    kn = kn_ref[...]
    for h in range(ATT_HEADS):
        cols = slice(h * Dh, (h + 1) * Dh)
        oq_ref[:, cols] = norm_rope(q_ref[:, cols], qn).astype(oq_ref.dtype)
        ok_ref[:, cols] = norm_rope(k_ref[:, cols], kn).astype(ok_ref.dtype)
    ov_ref[...] = v_ref[...].astype(ov_ref.dtype)


def _qkv_prep(proj, positions, qn_w, kn_w, B, S, ts=256):
    ts = _tile(S, ts)
    W = ATT_BRANCH
    G = len(DILATED_CONFIG)
    Dh = ATT_HEAD_DIM
    inv_freq = ROPE_THETA ** (-jnp.arange(0, ROPE_DIM, 2, dtype=F32) / ROPE_DIM)
    invf = jnp.zeros((Dh,), F32).at[:ROPE_DIM].set(jnp.tile(inv_freq, 2)).reshape(1, Dh)
    sec = lambda off: pl.BlockSpec((None, ts, W), lambda b, s, g, off=off: (b, s, off + g))
    out = pl.BlockSpec((None, None, ts, W), lambda b, s, g: (g, b, s, 0))
    nrm = pl.BlockSpec((None, 1, Dh), lambda b, s, g: (g, 0, 0))
    shp = jax.ShapeDtypeStruct((G, B, S, W), BF16)
    return pl.pallas_call(
        _qkv_prep_body,
        grid=(B, S // ts, G),
        in_specs=[pl.BlockSpec((None, ts, 1), lambda b, s, g: (b, s, 0)),
                  pl.BlockSpec((1, Dh), lambda b, s, g: (0, 0)),
                  sec(0), sec(G), sec(2 * G), nrm, nrm],
        out_specs=[out, out, out],
        out_shape=[shp, shp, shp],
        compiler_params=_params(("parallel", "parallel", "arbitrary")),
        name="qkv_prep",
    )(positions.reshape(B, S, 1), invf, proj, proj, proj,
      qn_w.reshape(G, 1, Dh), kn_w.reshape(G, 1, Dh))


def _attn_body(q_ref, k_ref, v_ref, o_ref, lse_ref, *, n_blocks, kw):
    Dh = ATT_HEAD_DIM
    BLK = ATT_BLOCK
    qt = pl.program_id(2)
    qi = lax.broadcasted_iota(jnp.int32, (BLK, kw), 0)
    kj = lax.broadcasted_iota(jnp.int32, (BLK, kw), 1)
    scale = Dh ** -0.5
    for i in range(n_blocks):
        n = qt * n_blocks + i
        if kw == BLK:
            start = 0
            off = 0
        else:
            start = pl.multiple_of(jnp.maximum(n - 1, 0) * BLK, BLK)
            off = jnp.where(n == 0, 0, BLK)
        mask = (kj >= qi + (off - BLK)) & (kj <= qi + off)
        rows = slice(i * BLK, (i + 1) * BLK)
        for h in range(ATT_HEADS):
            cols = slice(h * Dh, (h + 1) * Dh)
            s = _dot_t(q_ref[rows, cols], k_ref[pl.ds(start, kw), cols]) * scale
            s = jnp.where(mask, s, -jnp.inf)
            m = jnp.max(s, axis=-1, keepdims=True)
            p = jnp.exp(s - m)
            l = jnp.sum(p, axis=-1, keepdims=True)
            o = jnp.dot(p.astype(BF16), v_ref[pl.ds(start, kw), cols],
                        preferred_element_type=F32) / l
            o_ref[rows, cols] = o
            lse_ref[rows, cols] = jnp.broadcast_to(m + jnp.log(l), (BLK, Dh))


def _attn_group(q, k, v, dilation, B, S):
    W = ATT_BRANCH
    L = S // dilation
    view = lambda t: t.reshape(B, L, dilation * W)
    tq = _tile(L, 512)
    kw = min(2 * ATT_BLOCK, L)
    qspec = pl.BlockSpec((None, tq, W), lambda b, r, t: (b, t, r))
    kvspec = pl.BlockSpec((None, L, W), lambda b, r, t: (b, 0, r))
    shp = jax.ShapeDtypeStruct((B, L, dilation * W), F32)
    o, lse = pl.pallas_call(
        functools.partial(_attn_body, n_blocks=tq // ATT_BLOCK, kw=kw),
        grid=(B, dilation, L // tq),
        in_specs=[qspec, kvspec, kvspec],
        out_specs=[qspec, qspec],
        out_shape=[shp, shp],
        compiler_params=_params(("parallel", "parallel", "arbitrary")),
        name=f"dilated_attn_d{dilation}",
    )(view(q), view(k), view(v))
    return o.reshape(B * S, W), lse.reshape(B * S, W)


def _attn_combine_body(o0, o1, o2, l0, l1, l2, g_ref, out_ref):
    a, b, c = l0[...], l1[...], l2[...]
    m = jnp.maximum(jnp.maximum(a, b), c)
    ea, eb, ec = jnp.exp(a - m), jnp.exp(b - m), jnp.exp(c - m)
    den = ea + eb + ec
    o = (ea / den) * o0[...] + (eb / den) * o1[...] + (ec / den) * o2[...]
    g = g_ref[...]
    out_ref[...] = (o * (g * jax.nn.sigmoid(g))).astype(out_ref.dtype)


def _attn_combine(os_, lses, proj2d, tm=512):
    M, W = os_[0].shape
    tm = _tile(M, tm)
    tile = pl.BlockSpec((tm, W), lambda i: (i, 0))
    gate = pl.BlockSpec((tm, W), lambda i: (i, 3 * len(DILATED_CONFIG)))
    return pl.pallas_call(
        _attn_combine_body,
        grid=(M // tm,),
        in_specs=[tile] * 6 + [gate],
        out_specs=tile,
        out_shape=jax.ShapeDtypeStruct((M, W), BF16),
        compiler_params=_params(("parallel",)),
        name="attn_combine",
    )(*os_, *lses, proj2d)


def _attn_layer(x, positions, nw, w_in, qn_w, kn_w, w_out):
    B, S, D = x.shape
    M = B * S
    h = _norm(x.reshape(M, D), nw)
    proj = _matmul(h[None], w_in.astype(BF16)[None])[0]
    q, k, v = _qkv_prep(proj.reshape(B, S, -1), positions, qn_w, kn_w, B, S)
    os_, lses = [], []
    for gi, (_, dilation) in enumerate(DILATED_CONFIG):
        o, lse = _attn_group(q[gi], k[gi], v[gi], dilation, B, S)
        os_.append(o)
        lses.append(lse)
    og = _attn_combine(os_, lses, proj)
    x2 = _matmul(og[None], w_out.astype(BF16)[None], res=x.reshape(1, M, D))
    return x2.reshape(B, S, D)


def _pool_body(u_ref, up_ref, g_ref, wg_ref, sc_ref, o_ref):
    s = pl.program_id(1)
    ts = u_ref.shape[0]
    Wg = wg_ref.shape[-1]
    t = s * ts + lax.broadcasted_iota(jnp.int32, (ts, 1), 0)
    for j, win in enumerate(POOL_WINDOWS):
        cols = slice(j * Wg, (j + 1) * Wg)
        u = u_ref[:, cols]
        halo = jnp.where(s == 0, 0.0, up_ref[:, cols])
        acc = jnp.concatenate([halo, u], axis=0)
        step = 1
        while step < win:
            acc = acc + pltpu.roll(acc, step, axis=0)
            step *= 2
        cnt = jnp.minimum(t + 1, win).astype(F32)
        diff = acc[POOL_HALO:, :] / cnt - u
        y = jnp.dot(diff.astype(BF16), wg_ref[j], preferred_element_type=F32) * sc_ref[:, cols]
        g = g_ref[:, cols]
        o_ref[:, cols] = (y * (g * jax.nn.sigmoid(g))).astype(o_ref.dtype)


def _pool_mix(proj, w_grp, scale, B, S, ts=256):
    W = proj.shape[-1] // 2
    ts = _tile(S, ts)
    hb = ts // POOL_HALO
    return pl.pallas_call(
        _pool_body,
        grid=(B, S // ts),
        in_specs=[pl.BlockSpec((None, ts, W), lambda b, s: (b, s, 0)),
                  pl.BlockSpec((None, POOL_HALO, W),
                               lambda b, s: (b, jnp.maximum(s * hb - 1, 0), 0)),
                  pl.BlockSpec((None, ts, W), lambda b, s: (b, s, 1)),
                  pl.BlockSpec(w_grp.shape, lambda b, s: (0, 0, 0)),
                  pl.BlockSpec((1, W), lambda b, s: (0, 0))],
        out_specs=pl.BlockSpec((None, ts, W), lambda b, s: (b, s, 0)),
        out_shape=jax.ShapeDtypeStruct((B, S, W), BF16),
        compiler_params=_params(("parallel", "arbitrary")),
        name="pool_mix",
    )(proj, proj, proj, w_grp.astype(BF16), scale.reshape(1, W))


def _pool_layer(x, nw, w_in, w_grp, scale, w_out):
    B, S, D = x.shape
    M = B * S
    h = _norm(x.reshape(M, D), nw)
    proj = _matmul(h[None], w_in.astype(BF16)[None])[0]
    yg = _pool_mix(proj.reshape(B, S, -1), w_grp, scale, B, S)
    x2 = _matmul(yg.reshape(1, M, -1), w_out.astype(BF16)[None], res=x.reshape(1, M, D))
    return x2.reshape(B, S, D)


def kernel(x, positions, norm_w, a_mu, a_w_in, a_w0, a_w1, a_w2, a_a0, a_a1, a_a2, a_k_k, a_k_a,
           a_r_k, a_lnx_w, a_lnx_b, a_v0, a_v1, a_v2, a_w_out, b_w_in, b_qn_w, b_kn_w, b_w_out,
           c_w_in, c_w_grp, c_scale, c_w_out):
    depth = norm_w.shape[0]
    ia = ib = ic = 0
    v_first = None
    for i in range(depth):
        kind = i % 3
        if kind == 0:
            vres = None if ia == 0 else (a_v0[ia - 1], a_v1[ia - 1], a_v2[ia - 1])
            x, v_first = _rwkv_layer(
                x, norm_w[i], a_mu[ia], a_w_in[ia], a_w0[ia], a_w1[ia], a_w2[ia], a_a0[ia],
                a_a1[ia], a_a2[ia], a_k_k[ia], a_k_a[ia], a_r_k[ia].reshape(-1), a_lnx_w[ia],
                a_lnx_b[ia], a_w_out[ia], v_first, vres)
            ia += 1
        elif kind == 1:
            x = _attn_layer(x, positions, norm_w[i], b_w_in[ib], b_qn_w[ib], b_kn_w[ib],
                            b_w_out[ib])
            ib += 1
        else:
            x = _pool_layer(x, norm_w[i], c_w_in[ic], c_w_grp[ic], c_scale[ic], c_w_out[ic])
            ic += 1
    return x
```

```python
import functools

import jax
import jax.numpy as jnp
from jax import lax
from jax.experimental import pallas as pl
from jax.experimental.pallas import tpu as pltpu

F32 = jnp.float32
BF16 = jnp.bfloat16

RMS_EPS = 1e-6
RWKV_HEAD_DIM = 64
RWKV_LN_EPS = 64e-5
ATT_HEAD_DIM = 128
ATT_HEADS = 8
ATT_BRANCH = ATT_HEADS * ATT_HEAD_DIM
DILATED_CONFIG = ((128, 1), (512, 4), (2048, 16))
ATT_BLOCK = 128
ROPE_THETA = 500000.0
ROPE_DIM = ATT_HEAD_DIM // 4
POOL_WINDOWS = (2, 4, 8, 16)
POOL_HALO = 16

V7X_LANES = 128
V7X_MXU_DIM = 256
V7X_VMEM_BYTES = 64 * 1024 * 1024
VMEM_LIMIT = 48 * 1024 * 1024

SCAN_CHUNK = 64
SCAN_LANES = V7X_MXU_DIM
SCAN_HEADS = SCAN_LANES // RWKV_HEAD_DIM
SCAN_DOUBLINGS = 5


def _params(sem):
    return pltpu.CompilerParams(dimension_semantics=sem, vmem_limit_bytes=VMEM_LIMIT)


def _tile(n, pref):
    t = min(n, pref)
    while n % t:
        t //= 2
    return t


def _mm_body(a_ref, w_ref, o_ref):
    o_ref[...] = jnp.dot(a_ref[...], w_ref[...], preferred_element_type=F32).astype(o_ref.dtype)


def _mm_res_body(a_ref, w_ref, r_ref, o_ref):
    acc = jnp.dot(a_ref[...], w_ref[...], preferred_element_type=F32)
    o_ref[...] = (r_ref[...] + acc).astype(o_ref.dtype)


def _matmul(a, w, res=None, out_dtype=F32, tm=1024, tn=1024):
    _, M, K = a.shape
    G, _, N = w.shape
    tm = _tile(M, tm)
    tn = _tile(N, tn)
    in_specs = [pl.BlockSpec((None, tm, K), lambda g, i, j: (g, i, 0)),
                pl.BlockSpec((None, K, tn), lambda g, i, j: (g, 0, j))]
    args = [a, w]
    body = _mm_body
    if res is not None:
        in_specs.append(pl.BlockSpec((None, tm, tn), lambda g, i, j: (g, i, j)))
        args.append(res)
        body = _mm_res_body
    return pl.pallas_call(
        body,
        grid=(G, M // tm, N // tn),
        in_specs=in_specs,
        out_specs=pl.BlockSpec((None, tm, tn), lambda g, i, j: (g, i, j)),
        out_shape=jax.ShapeDtypeStruct((G, M, N), out_dtype),
        compiler_params=_params(("parallel", "parallel", "arbitrary")),
        name="matmul",
    )(*args)


def _rms(x, w):
    return (x * lax.rsqrt(jnp.mean(x * x, axis=-1, keepdims=True) + RMS_EPS)) * w


def _norm_body(x_ref, nw_ref, o_ref):
    o_ref[...] = _rms(x_ref[...], nw_ref[...]).astype(o_ref.dtype)


def _norm(x2d, nw, ts=512):
    M, D = x2d.shape
    ts = _tile(M, ts)
    return pl.pallas_call(
        _norm_body,
        grid=(M // ts,),
        in_specs=[pl.BlockSpec((ts, D), lambda i: (i, 0)),
                  pl.BlockSpec((1, D), lambda i: (0, 0))],
        out_specs=pl.BlockSpec((ts, D), lambda i: (i, 0)),
        out_shape=jax.ShapeDtypeStruct((M, D), BF16),
        compiler_params=_params(("parallel",)),
        name="rmsnorm",
    )(x2d, nw.reshape(1, D))


def _rwkv_prep_body(x_ref, xp_ref, nw_ref, mu_ref, o_ref):
    s = pl.program_id(1)
    nw = nw_ref[...]
    h = _rms(x_ref[...], nw)
    hp = _rms(xp_ref[...], nw)[7:8, :]
    hp = jnp.where(s == 0, 0.0, hp)
    row = lax.broadcasted_iota(jnp.int32, h.shape, 0)
    hs = jnp.where(row == 0, hp, pltpu.roll(h, 1, axis=0))
    xx = hs - h
    for j in range(6):
        o_ref[j] = (h + xx * mu_ref[j:j + 1, :]).astype(o_ref.dtype)


def _rwkv_prep(x, nw, mu, ts=256):
    B, S, D = x.shape
    ts = _tile(S, ts)
    sub = 8
    return pl.pallas_call(
        _rwkv_prep_body,
        grid=(B, S // ts),
        in_specs=[pl.BlockSpec((None, ts, D), lambda b, s: (b, s, 0)),
                  pl.BlockSpec((None, sub, D),
                               lambda b, s: (b, jnp.maximum(s * (ts // sub) - 1, 0), 0)),
                  pl.BlockSpec((1, D), lambda b, s: (0, 0)),
                  pl.BlockSpec((6, D), lambda b, s: (0, 0))],
        out_specs=pl.BlockSpec((6, None, ts, D), lambda b, s: (0, b, s, 0)),
        out_shape=jax.ShapeDtypeStruct((6, B, S, D), BF16),
        compiler_params=_params(("parallel", "arbitrary")),
        name="rwkv_prep",
    )(x, x, nw.reshape(1, D), mu)


def _rwkv_lora_body(*refs, has_v):
    if has_v:
        (xw_ref, xa_ref, xv_ref, v_ref, vf_ref,
         w1_ref, w2_ref, w0_ref, a1_ref, a2_ref, a0_ref, v1_ref, v2_ref, v0_ref,
         olw_ref, oa_ref, ov_ref) = refs
    else:
        (xw_ref, xa_ref, w1_ref, w2_ref, w0_ref, a1_ref, a2_ref, a0_ref,
         olw_ref, oa_ref) = refs
    t = jnp.tanh(jnp.dot(xw_ref[...], w1_ref[...], preferred_element_type=F32))
    z = w0_ref[...] + jnp.dot(t.astype(BF16), w2_ref[...], preferred_element_type=F32)
    w_log = -jax.nn.softplus(-z) - 0.5
    olw_ref[...] = -jnp.exp(w_log)
    ta = jnp.dot(xa_ref[...], a1_ref[...], preferred_element_type=F32)
    oa_ref[...] = jax.nn.sigmoid(
        a0_ref[...] + jnp.dot(ta.astype(BF16), a2_ref[...], preferred_element_type=F32))
    if has_v:
        tv = jnp.dot(xv_ref[...], v1_ref[...], preferred_element_type=F32)
        sv = jax.nn.sigmoid(
            v0_ref[...] + jnp.dot(tv.astype(BF16), v2_ref[...], preferred_element_type=F32))
        v = v_ref[...]
        ov_ref[...] = v + (vf_ref[...] - v) * sv


def _pad_lora(w1, w2):
    r = w1.shape[1]
    rp = -(-r // V7X_LANES) * V7X_LANES
    return (jnp.pad(w1, ((0, 0), (0, rp - r))).astype(BF16),
            jnp.pad(w2, ((0, rp - r), (0, 0))).astype(BF16))


def _rwkv_lora(mix, proj, v_first, w0, w1, w2, a0, a1, a2, vres, tm=256):
    _, M, D = mix.shape
    tm = _tile(M, tm)
    has_v = vres is not None
    row = lambda j: pl.BlockSpec((None, tm, D), lambda i, j=j: (j, i, 0))
    full = lambda a: pl.BlockSpec(a.shape, lambda i: (0,) * a.ndim)
    tile = pl.BlockSpec((tm, D), lambda i: (i, 0))
    w1p, w2p = _pad_lora(w1, w2)
    a1p, a2p = _pad_lora(a1, a2)
    weights = [w1p, w2p, w0.reshape(1, D), a1p, a2p, a0.reshape(1, D)]
    args = [mix, mix]
    in_specs = [row(4), row(5)]
    n_out = 2
    if has_v:
        v0, v1, v2 = vres
        v1p, v2p = _pad_lora(v1, v2)
        weights += [v1p, v2p, v0.reshape(1, D)]
        args += [mix, proj, v_first]
        in_specs += [row(2), row(2), row(2)]
        n_out = 3
    args += weights
    in_specs += [full(w) for w in weights]
    return pl.pallas_call(
        functools.partial(_rwkv_lora_body, has_v=has_v),
        grid=(M // tm,),
        in_specs=in_specs,
        out_specs=[tile] * n_out,
        out_shape=[jax.ShapeDtypeStruct((M, D), F32)] * n_out,
        compiler_params=_params(("parallel",)),
        name="rwkv_lora",
    )(*args)


def _split2(x):
    hi = x.astype(BF16)
    return hi, (x - hi.astype(F32)).astype(BF16)


def _dot_t(a, b):
    return lax.dot_general(a, b, (((1,), (1,)), ((), ())), preferred_element_type=F32)


def _dot_tl(a, b):
    return lax.dot_general(a, b, (((0,), (0,)), ((), ())), preferred_element_type=F32)


def _scan_body(r_ref, k_ref, v_ref, lw_ref, a_ref, g_ref,
               kk_ref, ka_ref, rk_ref, lnw_ref, lnb_ref, o_ref, state_ref, *, n_chunks):
    C, L, H, N = SCAN_CHUNK, SCAN_LANES, SCAN_HEADS, RWKV_HEAD_DIM
    n_streams = state_ref.shape[0]
    streams = range(n_streams)

    @pl.when(pl.program_id(2) == 0)
    def _():
        state_ref[...] = jnp.zeros_like(state_ref)

    lane_head = lax.broadcasted_iota(jnp.int32, (C, L), 1) // N
    head_masks = [lane_head == h for h in range(H)]
    t_idx = lax.broadcasted_iota(jnp.int32, (C, H * C), 0)
    s_idx = lax.broadcasted_iota(jnp.int32, (C, H * C), 1) % C
    strict_lower = s_idx < t_idx
    lower = s_idx <= t_idx
    eye = (s_idx == t_idx).astype(F32)
    bd_mask = (lax.broadcasted_iota(jnp.int32, (L, L), 0) // N
               == lax.broadcasted_iota(jnp.int32, (L, L), 1) // N)
    ones_bd = bd_mask.astype(BF16)
    tri = (lax.broadcasted_iota(jnp.int32, (C, C), 1)
           <= lax.broadcasted_iota(jnp.int32, (C, C), 0)).astype(BF16)
    hc = H * C
    inv_n = 1.0 / N

    def blockstack(x):
        return jnp.concatenate([jnp.where(m, x, 0.0) for m in head_masks], axis=0).astype(BF16)

    def segsum(xs):
        parts = jnp.concatenate([p for x in xs for p in _split2(x)], axis=0)
        s = jnp.dot(parts, ones_bd, preferred_element_type=F32)
        return [s[2 * g * C:(2 * g + 1) * C] + s[(2 * g + 1) * C:(2 * g + 2) * C]
                for g in range(len(xs))]

    def mm(a, b):
        return jnp.dot(a.astype(BF16), b, preferred_element_type=F32)

    lanes = lambda g: slice(g * L, (g + 1) * L)
    split = lambda ref: [ref[:, lanes(g)] for g in streams]
    k_k, k_a, r_k, ln_w, ln_b = (split(ref) for ref in (kk_ref, ka_ref, rk_ref, lnw_ref, lnb_ref))

    def chunk(ci, carry):
        sl = pl.ds(pl.multiple_of(ci * C, C), C)
        load = lambda ref: [ref[sl, lanes(g)] for g in streams]
        r, k, v, lw, a, gate = (load(ref) for ref in (r_ref, k_ref, v_ref, lw_ref, a_ref, g_ref))

        kk = [k[g] * k_k[g] for g in streams]
        n2 = segsum([x * x for x in kk])
        kk = [kk[g] / jnp.maximum(jnp.sqrt(n2[g]), 1e-12) for g in streams]
        k2 = [k[g] * (1.0 + (a[g] - 1.0) * k_a[g]) for g in streams]
        b_s = [kk[g] * a[g] for g in streams]

        cs = [jnp.dot(tri, jnp.concatenate(_split2(lw[g]), axis=1), preferred_element_type=F32)
              for g in streams]
        cum = [c[:, :L] + c[:, L:] for c in cs]
        total = [c[C - 1:C, :] for c in cum]
        at = [-kk[g] * jnp.exp(cum[g] - lw[g]) for g in streams]
        rt = [r[g] * jnp.exp(cum[g]) for g in streams]
        w_inv = [jnp.exp(-c) for c in cum]
        w_tail = [jnp.exp(total[g] - cum[g]) for g in streams]

        aa = [_dot_t(jnp.concatenate([at[g], rt[g]], axis=0).astype(BF16),
                     jnp.concatenate([blockstack(b_s[g] * w_inv[g]),
                                      blockstack(k2[g] * w_inv[g])], axis=0))
              for g in streams]
        a_ab = [jnp.where(strict_lower, x[:C, :hc], 0.0) for x in aa]
        a_ak = [jnp.where(strict_lower, x[:C, hc:], 0.0) for x in aa]
        a_rr = [jnp.concatenate([jnp.where(lower, x[C:, :hc], 0.0),
                                 jnp.where(lower, x[C:, hc:], 0.0)], axis=1) for x in aa]

        t_inv = [eye + x for x in a_ab]
        p = [mm(x, blockstack(x)) for x in a_ab]
        for i in range(SCAN_DOUBLINGS):
            bd = [blockstack(x) for x in p]
            if i + 1 < SCAN_DOUBLINGS:
                pt = [mm(jnp.concatenate([p[g], t_inv[g]], axis=0), bd[g]) for g in streams]
                p = [x[:C] for x in pt]
                t_inv = [t_inv[g] + pt[g][C:] for g in streams]
            else:
                t_inv = [t_inv[g] + mm(t_inv[g], bd[g]) for g in streams]

        state = [state_ref[g] for g in streams]
        st = [x.astype(BF16) for x in state]
        bs_v = [blockstack(x) for x in v]
        z = [_dot_t(at[g].astype(BF16), st[g]) + mm(a_ak[g], bs_v[g]) for g in streams]
        u = [mm(t_inv[g], blockstack(z[g])) for g in streams]
        y = [_dot_t(rt[g].astype(BF16), st[g])
             + mm(a_rr[g], jnp.concatenate([blockstack(u[g]), bs_v[g]], axis=0))
             for g in streams]
        upd = [_dot_tl(jnp.concatenate([u[g], v[g]], axis=0).astype(BF16),
                       jnp.concatenate([b_s[g] * w_tail[g], k2[g] * w_tail[g]],
                                       axis=0).astype(BF16)) for g in streams]
        for g in streams:
            state_ref[g] = state[g] * jnp.exp(total[g]) + jnp.where(bd_mask, upd[g], 0.0)

        mean = segsum(y)
        yc = [y[g] - mean[g] * inv_n for g in streams]
        var = segsum([x * x for x in yc])
        bonus = segsum([r[g] * k2[g] * r_k[g] for g in streams])
        for g in streams:
            yn = (yc[g] * lax.rsqrt(var[g] * inv_n + RWKV_LN_EPS)) * ln_w[g] + ln_b[g]
            out = (yn + bonus[g] * v[g]) * (gate[g] * jax.nn.sigmoid(gate[g]))
            o_ref[sl, lanes(g)] = out.astype(o_ref.dtype)
        return carry

    lax.fori_loop(0, n_chunks, chunk, 0)


def _rwkv_scan(proj, v, v_idx, lw, a, k_k, k_a, r_k, ln_w, ln_b, B, S, ts=256, n_streams=8):
    _, M, D = proj.shape
    ts = _tile(S, ts)
    L = SCAN_LANES
    W = n_streams * L
    nt = S // ts
    pj = lambda j: pl.BlockSpec((None, ts, W), lambda b, h, c, j=j: (j, b * nt + c, h))
    tile = pl.BlockSpec((ts, W), lambda b, h, c: (b * nt + c, h))
    par = pl.BlockSpec((1, W), lambda b, h, c: (0, h))
    vec = lambda p: p.reshape(1, D)
    return pl.pallas_call(
        functools.partial(_scan_body, n_chunks=ts // SCAN_CHUNK),
        grid=(B, D // W, nt),
        in_specs=[pj(0), pj(1), pj(v_idx), tile, tile, pj(3), par, par, par, par, par],
        out_specs=tile,
        out_shape=jax.ShapeDtypeStruct((M, D), BF16),
        scratch_shapes=[pltpu.VMEM((n_streams, L, L), F32)],
        compiler_params=_params(("parallel", "parallel", "arbitrary")),
        name="rwkv_scan",
    )(proj, proj, v, lw, a, proj, vec(k_k), vec(k_a), vec(r_k), vec(ln_w), vec(ln_b))


def _rwkv_layer(x, nw, mu, w_in, w0, w1, w2, a0, a1, a2, k_k, k_a, r_k, ln_w, ln_b, w_out,
                v_first, vres):
    B, S, D = x.shape
    M = B * S
    mix = _rwkv_prep(x, nw, mu).reshape(6, M, D)
    proj = _matmul(mix, w_in.astype(BF16))
    outs = _rwkv_lora(mix, proj, v_first, w0, w1, w2, a0, a1, a2, vres)
    if vres is None:
        lw, a = outs
        v, v_idx = proj, 2
        v_first = proj
    else:
        lw, a, v = outs
        v, v_idx = v[None], 0
    yg = _rwkv_scan(proj, v, v_idx, lw, a, k_k, k_a, r_k, ln_w, ln_b, B, S)
    x2 = _matmul(yg[None], w_out.astype(BF16)[None], res=x.reshape(1, M, D))
    return x2.reshape(B, S, D), v_first


def _qkv_prep_body(pos_ref, invf_ref, qn_ref, kn_ref, *refs):
    G = len(DILATED_CONFIG)
    q_refs, k_refs, v_refs = refs[:G], refs[G:2 * G], refs[2 * G:3 * G]
    oq_refs, ok_refs, ov_refs = refs[3 * G:4 * G], refs[4 * G:5 * G], refs[5 * G:]
    Dh = ATT_HEAD_DIM
    half = ROPE_DIM // 2
    ang = pos_ref[...].astype(F32) * invf_ref[...]
    cos = jnp.cos(ang)
    sin = jnp.sin(ang)
    lane = lax.broadcasted_iota(jnp.int32, ang.shape, 1)
    s_lo = jnp.where(lane < half, -sin, 0.0)
    s_hi = jnp.where((lane >= half) & (lane < ROPE_DIM), sin, 0.0)

    def norm_rope(x, w):
        xn = _rms(x, w)
        return (xn * cos + pltpu.roll(xn, Dh - half, axis=1) * s_lo
                + pltpu.roll(xn, half, axis=1) * s_hi)

    for g in range(G):
        qn = qn_ref[g]
        kn = kn_ref[g]
        for h in range(ATT_HEADS):
            cols = slice(h * Dh, (h + 1) * Dh)
            oq_refs[g][:, cols] = norm_rope(q_refs[g][:, cols], qn).astype(BF16)
            ok_refs[g][:, cols] = norm_rope(k_refs[g][:, cols], kn).astype(BF16)
        ov_refs[g][...] = v_refs[g][...].astype(BF16)


def _qkv_prep(proj, positions, qn_w, kn_w, B, S, ts=256):
    ts = _tile(S, ts)
    W = ATT_BRANCH
    G = len(DILATED_CONFIG)
    Dh = ATT_HEAD_DIM
    inv_freq = ROPE_THETA ** (-jnp.arange(0, ROPE_DIM, 2, dtype=F32) / ROPE_DIM)
    invf = jnp.zeros((Dh,), F32).at[:ROPE_DIM].set(jnp.tile(inv_freq, 2)).reshape(1, Dh)
    sec = lambda c: pl.BlockSpec((None, ts, W), lambda b, s, c=c: (b, s, c))
    out = pl.BlockSpec((None, ts, W), lambda b, s: (b, s, 0))
    nrm = pl.BlockSpec((G, 1, Dh), lambda b, s: (0, 0, 0))
    shp = jax.ShapeDtypeStruct((B, S, W), BF16)
    outs = pl.pallas_call(
        _qkv_prep_body,
        grid=(B, S // ts),
        in_specs=[pl.BlockSpec((None, ts, 1), lambda b, s: (b, s, 0)),
                  pl.BlockSpec((1, Dh), lambda b, s: (0, 0)), nrm, nrm]
                 + [sec(c) for c in range(3 * G)],
        out_specs=[out] * (3 * G),
        out_shape=[shp] * (3 * G),
        compiler_params=_params(("parallel", "arbitrary")),
        name="qkv_prep",
    )(positions.reshape(B, S, 1), invf, qn_w.reshape(G, 1, Dh), kn_w.reshape(G, 1, Dh),
      *([proj] * (3 * G)))
    return outs[:G], outs[G:2 * G], outs[2 * G:]


def _attn_body(q_ref, k_ref, v_ref, o_ref, lse_ref, *, n_blocks, kw):
    Dh = ATT_HEAD_DIM
    BLK = ATT_BLOCK
    qt = pl.program_id(2)
    qi = lax.broadcasted_iota(jnp.int32, (BLK, kw), 0)
    kj = lax.broadcasted_iota(jnp.int32, (BLK, kw), 1)
    scale = Dh ** -0.5
    for i in range(n_blocks):
        n = qt * n_blocks + i
        if kw == BLK:
            start = 0
            off = 0
        else:
            start = pl.multiple_of(jnp.maximum(n - 1, 0) * BLK, BLK)
            off = jnp.where(n == 0, 0, BLK)
        mask = (kj >= qi + (off - BLK)) & (kj <= qi + off)
        rows = slice(i * BLK, (i + 1) * BLK)
        for h in range(ATT_HEADS):
            cols = slice(h * Dh, (h + 1) * Dh)
            s = _dot_t(q_ref[rows, cols], k_ref[pl.ds(start, kw), cols]) * scale
            s = jnp.where(mask, s, -jnp.inf)
            m = jnp.max(s, axis=-1, keepdims=True)
            p = jnp.exp(s - m)
            l = jnp.sum(p, axis=-1, keepdims=True)
            o = jnp.dot(p.astype(BF16), v_ref[pl.ds(start, kw), cols],
                        preferred_element_type=F32) / l
            o_ref[rows, cols] = o
            lse_ref[rows, cols] = jnp.broadcast_to(m + jnp.log(l), (BLK, Dh))


def _attn_group(q, k, v, dilation, B, S):
    W = ATT_BRANCH
    L = S // dilation
    view = lambda t: t.reshape(B, L, dilation * W)
    tq = _tile(L, 512)
    kw = min(2 * ATT_BLOCK, L)
    qspec = pl.BlockSpec((None, tq, W), lambda b, r, t: (b, t, r))
    kvspec = pl.BlockSpec((None, L, W), lambda b, r, t: (b, 0, r))
    shp = jax.ShapeDtypeStruct((B, L, dilation * W), F32)
    o, lse = pl.pallas_call(
        functools.partial(_attn_body, n_blocks=tq // ATT_BLOCK, kw=kw),
        grid=(B, dilation, L // tq),
        in_specs=[qspec, kvspec, kvspec],
        out_specs=[qspec, qspec],
        out_shape=[shp, shp],
        compiler_params=_params(("parallel", "parallel", "arbitrary")),
        name=f"dilated_attn_d{dilation}",
    )(view(q), view(k), view(v))
    return o.reshape(B * S, W), lse.reshape(B * S, W)


def _attn_combine_body(o0, o1, o2, l0, l1, l2, g_ref, out_ref):
    a, b, c = l0[...], l1[...], l2[...]
    m = jnp.maximum(jnp.maximum(a, b), c)
    ea, eb, ec = jnp.exp(a - m), jnp.exp(b - m), jnp.exp(c - m)
    den = ea + eb + ec
    o = (ea / den) * o0[...] + (eb / den) * o1[...] + (ec / den) * o2[...]
    g = g_ref[...]
    out_ref[...] = (o * (g * jax.nn.sigmoid(g))).astype(out_ref.dtype)


def _attn_combine(os_, lses, proj2d, tm=512):
    M, W = os_[0].shape
    tm = _tile(M, tm)
    tile = pl.BlockSpec((tm, W), lambda i: (i, 0))
    gate = pl.BlockSpec((tm, W), lambda i: (i, 3 * len(DILATED_CONFIG)))
    return pl.pallas_call(
        _attn_combine_body,
        grid=(M // tm,),
        in_specs=[tile] * 6 + [gate],
        out_specs=tile,
        out_shape=jax.ShapeDtypeStruct((M, W), BF16),
        compiler_params=_params(("parallel",)),
        name="attn_combine",
    )(*os_, *lses, proj2d)


def _attn_layer(x, positions, nw, w_in, qn_w, kn_w, w_out):
    B, S, D = x.shape
    M = B * S
    h = _norm(x.reshape(M, D), nw)
    proj = _matmul(h[None], w_in.astype(BF16)[None])[0]
    q, k, v = _qkv_prep(proj.reshape(B, S, -1), positions, qn_w, kn_w, B, S)
    os_, lses = [], []
    for gi, (_, dilation) in enumerate(DILATED_CONFIG):
        o, lse = _attn_group(q[gi], k[gi], v[gi], dilation, B, S)
        os_.append(o)
        lses.append(lse)
    og = _attn_combine(os_, lses, proj)
    x2 = _matmul(og[None], w_out.astype(BF16)[None], res=x.reshape(1, M, D))
    return x2.reshape(B, S, D)


def _pool_body(u_ref, up_ref, g_ref, wg_ref, sc_ref, o_ref):
    s = pl.program_id(1)
    ts = u_ref.shape[0]
    Wg = wg_ref.shape[-1]
    t = s * ts + lax.broadcasted_iota(jnp.int32, (ts, 1), 0)
    for j, win in enumerate(POOL_WINDOWS):
        cols = slice(j * Wg, (j + 1) * Wg)
        u = u_ref[:, cols]
        halo = jnp.where(s == 0, 0.0, up_ref[:, cols])
        acc = jnp.concatenate([halo, u], axis=0)
        step = 1
        while step < win:
            acc = acc + pltpu.roll(acc, step, axis=0)
            step *= 2
        cnt = jnp.minimum(t + 1, win).astype(F32)
        diff = acc[POOL_HALO:, :] / cnt - u
        y = jnp.dot(diff.astype(BF16), wg_ref[j], preferred_element_type=F32) * sc_ref[:, cols]
        g = g_ref[:, cols]
        o_ref[:, cols] = (y * (g * jax.nn.sigmoid(g))).astype(o_ref.dtype)


def _pool_mix(proj, w_grp, scale, B, S, ts=256):
    W = proj.shape[-1] // 2
    ts = _tile(S, ts)
    hb = ts // POOL_HALO
    return pl.pallas_call(
        _pool_body,
        grid=(B, S // ts),
        in_specs=[pl.BlockSpec((None, ts, W), lambda b, s: (b, s, 0)),
                  pl.BlockSpec((None, POOL_HALO, W),
                               lambda b, s: (b, jnp.maximum(s * hb - 1, 0), 0)),
                  pl.BlockSpec((None, ts, W), lambda b, s: (b, s, 1)),
                  pl.BlockSpec(w_grp.shape, lambda b, s: (0, 0, 0)),
                  pl.BlockSpec((1, W), lambda b, s: (0, 0))],
        out_specs=pl.BlockSpec((None, ts, W), lambda b, s: (b, s, 0)),
        out_shape=jax.ShapeDtypeStruct((B, S, W), BF16),
        compiler_params=_params(("parallel", "arbitrary")),
        name="pool_mix",
    )(proj, proj, proj, w_grp.astype(BF16), scale.reshape(1, W))


def _pool_layer(x, nw, w_in, w_grp, scale, w_out):
    B, S, D = x.shape
    M = B * S
    h = _norm(x.reshape(M, D), nw)
    proj = _matmul(h[None], w_in.astype(BF16)[None])[0]
    yg = _pool_mix(proj.reshape(B, S, -1), w_grp, scale, B, S)
    x2 = _matmul(yg.reshape(1, M, -1), w_out.astype(BF16)[None], res=x.reshape(1, M, D))
    return x2.reshape(B, S, D)


def kernel(x, positions, norm_w, a_mu, a_w_in, a_w0, a_w1, a_w2, a_a0, a_a1, a_a2, a_k_k, a_k_a,
           a_r_k, a_lnx_w, a_lnx_b, a_v0, a_v1, a_v2, a_w_out, b_w_in, b_qn_w, b_kn_w, b_w_out,
           c_w_in, c_w_grp, c_scale, c_w_out):
    depth = norm_w.shape[0]
    ia = ib = ic = 0
    v_first = None
    for i in range(depth):
        kind = i % 3
        if kind == 0:
            vres = None if ia == 0 else (a_v0[ia - 1], a_v1[ia - 1], a_v2[ia - 1])
            x, v_first = _rwkv_layer(
                x, norm_w[i], a_mu[ia], a_w_in[ia], a_w0[ia], a_w1[ia], a_w2[ia], a_a0[ia],
                a_a1[ia], a_a2[ia], a_k_k[ia], a_k_a[ia], a_r_k[ia].reshape(-1), a_lnx_w[ia],
                a_lnx_b[ia], a_w_out[ia], v_first, vres)
            ia += 1
        elif kind == 1:
            x = _attn_layer(x, positions, norm_w[i], b_w_in[ib], b_qn_w[ib], b_kn_w[ib],
                            b_w_out[ib])
            ib += 1
        else:
            x = _pool_layer(x, norm_w[i], c_w_in[ic], c_w_grp[ic], c_scale[ic], c_w_out[ic])
            ic += 1
    return x
```

```python
import functools

import jax
import jax.numpy as jnp
from jax import lax
from jax.experimental import pallas as pl
from jax.experimental.pallas import tpu as pltpu

F32 = jnp.float32
BF16 = jnp.bfloat16

RMS_EPS = 1e-6
RWKV_HEAD_DIM = 64
RWKV_LN_EPS = 64e-5
ATT_HEAD_DIM = 128
ATT_HEADS = 8
ATT_BRANCH = ATT_HEADS * ATT_HEAD_DIM
DILATED_CONFIG = ((128, 1), (512, 4), (2048, 16))
ATT_BLOCK = 128
ROPE_THETA = 500000.0
ROPE_DIM = ATT_HEAD_DIM // 4
POOL_WINDOWS = (2, 4, 8, 16)
POOL_HALO = 16

V7X_LANES = 128
V7X_MXU_DIM = 256
V7X_VMEM_BYTES = 64 * 1024 * 1024
VMEM_LIMIT = 48 * 1024 * 1024

SCAN_CHUNK = 64
SCAN_LANES = V7X_MXU_DIM
SCAN_HEADS = SCAN_LANES // RWKV_HEAD_DIM
SCAN_DOUBLINGS = 5


def _params(sem):
    return pltpu.CompilerParams(dimension_semantics=sem, vmem_limit_bytes=VMEM_LIMIT)


def _tile(n, pref):
    t = min(n, pref)
    while n % t:
        t //= 2
    return t


def _mm_body(a_ref, w_ref, o_ref):
    o_ref[...] = jnp.dot(a_ref[...], w_ref[...], preferred_element_type=F32).astype(o_ref.dtype)


def _mm_res_body(a_ref, w_ref, r_ref, o_ref):
    acc = jnp.dot(a_ref[...], w_ref[...], preferred_element_type=F32)
    o_ref[...] = (r_ref[...] + acc).astype(o_ref.dtype)


def _matmul(a, w, res=None, out_dtype=F32, tm=1024, tn=1024):
    _, M, K = a.shape
    G, _, N = w.shape
    tm = _tile(M, tm)
    tn = _tile(N, tn)
    in_specs = [pl.BlockSpec((None, tm, K), lambda g, i, j: (g, i, 0)),
                pl.BlockSpec((None, K, tn), lambda g, i, j: (g, 0, j))]
    args = [a, w]
    body = _mm_body
    if res is not None:
        in_specs.append(pl.BlockSpec((None, tm, tn), lambda g, i, j: (g, i, j)))
        args.append(res)
        body = _mm_res_body
    return pl.pallas_call(
        body,
        grid=(G, M // tm, N // tn),
        in_specs=in_specs,
        out_specs=pl.BlockSpec((None, tm, tn), lambda g, i, j: (g, i, j)),
        out_shape=jax.ShapeDtypeStruct((G, M, N), out_dtype),
        compiler_params=_params(("parallel", "parallel", "arbitrary")),
        name="matmul",
    )(*args)


def _rms(x, w):
    return (x * lax.rsqrt(jnp.mean(x * x, axis=-1, keepdims=True) + RMS_EPS)) * w


def _norm_body(x_ref, nw_ref, o_ref):
    o_ref[...] = _rms(x_ref[...], nw_ref[...]).astype(o_ref.dtype)


def _norm(x2d, nw, ts=512):
    M, D = x2d.shape
    ts = _tile(M, ts)
    return pl.pallas_call(
        _norm_body,
        grid=(M // ts,),
        in_specs=[pl.BlockSpec((ts, D), lambda i: (i, 0)),
                  pl.BlockSpec((1, D), lambda i: (0, 0))],
        out_specs=pl.BlockSpec((ts, D), lambda i: (i, 0)),
        out_shape=jax.ShapeDtypeStruct((M, D), BF16),
        compiler_params=_params(("parallel",)),
        name="rmsnorm",
    )(x2d, nw.reshape(1, D))


def _rwkv_prep_body(x_ref, xp_ref, nw_ref, mu_ref, o_ref):
    s = pl.program_id(1)
    nw = nw_ref[...]
    h = _rms(x_ref[...], nw)
    hp = _rms(xp_ref[...], nw)[7:8, :]
    hp = jnp.where(s == 0, 0.0, hp)
    row = lax.broadcasted_iota(jnp.int32, h.shape, 0)
    hs = jnp.where(row == 0, hp, pltpu.roll(h, 1, axis=0))
    xx = hs - h
    for j in range(6):
        o_ref[j] = (h + xx * mu_ref[j:j + 1, :]).astype(o_ref.dtype)


def _rwkv_prep(x, nw, mu, ts=256):
    B, S, D = x.shape
    ts = _tile(S, ts)
    sub = 8
    return pl.pallas_call(
        _rwkv_prep_body,
        grid=(B, S // ts),
        in_specs=[pl.BlockSpec((None, ts, D), lambda b, s: (b, s, 0)),
                  pl.BlockSpec((None, sub, D),
                               lambda b, s: (b, jnp.maximum(s * (ts // sub) - 1, 0), 0)),
                  pl.BlockSpec((1, D), lambda b, s: (0, 0)),
                  pl.BlockSpec((6, D), lambda b, s: (0, 0))],
        out_specs=pl.BlockSpec((6, None, ts, D), lambda b, s: (0, b, s, 0)),
        out_shape=jax.ShapeDtypeStruct((6, B, S, D), BF16),
        compiler_params=_params(("parallel", "arbitrary")),
        name="rwkv_prep",
    )(x, x, nw.reshape(1, D), mu)


def _rwkv_lora_body(*refs, has_v):
    if has_v:
        (xw_ref, xa_ref, xv_ref, v_ref, vf_ref,
         w1_ref, w2_ref, w0_ref, a1_ref, a2_ref, a0_ref, v1_ref, v2_ref, v0_ref,
         olw_ref, oa_ref, ov_ref) = refs
    else:
        (xw_ref, xa_ref, w1_ref, w2_ref, w0_ref, a1_ref, a2_ref, a0_ref,
         olw_ref, oa_ref) = refs
    t = jnp.tanh(jnp.dot(xw_ref[...], w1_ref[...], preferred_element_type=F32))
    z = w0_ref[...] + jnp.dot(t.astype(BF16), w2_ref[...], preferred_element_type=F32)
    w_log = -jax.nn.softplus(-z) - 0.5
    olw_ref[...] = -jnp.exp(w_log)
    ta = jnp.dot(xa_ref[...], a1_ref[...], preferred_element_type=F32)
    oa_ref[...] = jax.nn.sigmoid(
        a0_ref[...] + jnp.dot(ta.astype(BF16), a2_ref[...], preferred_element_type=F32))
    if has_v:
        tv = jnp.dot(xv_ref[...], v1_ref[...], preferred_element_type=F32)
        sv = jax.nn.sigmoid(
            v0_ref[...] + jnp.dot(tv.astype(BF16), v2_ref[...], preferred_element_type=F32))
        v = v_ref[...]
        ov_ref[...] = v + (vf_ref[...] - v) * sv


def _pad_lora(w1, w2):
    r = w1.shape[1]
    rp = -(-r // V7X_LANES) * V7X_LANES
    return (jnp.pad(w1, ((0, 0), (0, rp - r))).astype(BF16),
            jnp.pad(w2, ((0, rp - r), (0, 0))).astype(BF16))


def _rwkv_lora(mix, proj, v_first, w0, w1, w2, a0, a1, a2, vres, tm=256):
    _, M, D = mix.shape
    tm = _tile(M, tm)
    has_v = vres is not None
    row = lambda j: pl.BlockSpec((None, tm, D), lambda i, j=j: (j, i, 0))
    full = lambda a: pl.BlockSpec(a.shape, lambda i: (0,) * a.ndim)
    tile = pl.BlockSpec((tm, D), lambda i: (i, 0))
    w1p, w2p = _pad_lora(w1, w2)
    a1p, a2p = _pad_lora(a1, a2)
    weights = [w1p, w2p, w0.reshape(1, D), a1p, a2p, a0.reshape(1, D)]
    args = [mix, mix]
    in_specs = [row(4), row(5)]
    n_out = 2
    if has_v:
        v0, v1, v2 = vres
        v1p, v2p = _pad_lora(v1, v2)
        weights += [v1p, v2p, v0.reshape(1, D)]
        args += [mix, proj, v_first]
        in_specs += [row(2), row(2), row(2)]
        n_out = 3
    args += weights
    in_specs += [full(w) for w in weights]
    return pl.pallas_call(
        functools.partial(_rwkv_lora_body, has_v=has_v),
        grid=(M // tm,),
        in_specs=in_specs,
        out_specs=[tile] * n_out,
        out_shape=[jax.ShapeDtypeStruct((M, D), F32)] * n_out,
        compiler_params=_params(("parallel",)),
        name="rwkv_lora",
    )(*args)


def _split2(x):
    hi = x.astype(BF16)
    return hi, (x - hi.astype(F32)).astype(BF16)


def _dot_t(a, b):
    return lax.dot_general(a, b, (((1,), (1,)), ((), ())), preferred_element_type=F32)


def _dot_tl(a, b):
    return lax.dot_general(a, b, (((0,), (0,)), ((), ())), preferred_element_type=F32)


def _scan_body(r_ref, k_ref, v_ref, lw_ref, a_ref, g_ref,
               kk_ref, ka_ref, rk_ref, lnw_ref, lnb_ref, o_ref, state_ref, *, n_chunks):
    C, L, H, N = SCAN_CHUNK, SCAN_LANES, SCAN_HEADS, RWKV_HEAD_DIM
    n_streams = state_ref.shape[0]
    streams = range(n_streams)

    @pl.when(pl.program_id(2) == 0)
    def _():
        state_ref[...] = jnp.zeros_like(state_ref)

    lane_head = lax.broadcasted_iota(jnp.int32, (C, L), 1) // N
    head_masks = [lane_head == h for h in range(H)]
    t_idx = lax.broadcasted_iota(jnp.int32, (C, H * C), 0)
    s_idx = lax.broadcasted_iota(jnp.int32, (C, H * C), 1) % C
    strict_lower = s_idx < t_idx
    lower = s_idx <= t_idx
    eye = (s_idx == t_idx).astype(F32)
    bd_mask = (lax.broadcasted_iota(jnp.int32, (L, L), 0) // N
               == lax.broadcasted_iota(jnp.int32, (L, L), 1) // N)
    ones_bd = bd_mask.astype(BF16)
    tri = (lax.broadcasted_iota(jnp.int32, (C, C), 1)
           <= lax.broadcasted_iota(jnp.int32, (C, C), 0)).astype(BF16)
    hc = H * C
    inv_n = 1.0 / N

    def blockstack(x):
        return jnp.concatenate([jnp.where(m, x, 0.0) for m in head_masks], axis=0).astype(BF16)

    def segsum(xs):
        parts = jnp.concatenate([p for x in xs for p in _split2(x)], axis=0)
        s = jnp.dot(parts, ones_bd, preferred_element_type=F32)
        return [s[2 * g * C:(2 * g + 1) * C] + s[(2 * g + 1) * C:(2 * g + 2) * C]
                for g in range(len(xs))]

    def mm(a, b):
        return jnp.dot(a.astype(BF16), b, preferred_element_type=F32)

    lanes = lambda g: slice(g * L, (g + 1) * L)
    split = lambda ref: [ref[:, lanes(g)] for g in streams]
    k_k, k_a, r_k, ln_w, ln_b = (split(ref) for ref in (kk_ref, ka_ref, rk_ref, lnw_ref, lnb_ref))

    def chunk(ci, carry):
        sl = pl.ds(pl.multiple_of(ci * C, C), C)
        load = lambda ref: [ref[sl, lanes(g)] for g in streams]
        r, k, v, lw, a, gate = (load(ref) for ref in (r_ref, k_ref, v_ref, lw_ref, a_ref, g_ref))

        kk = [k[g] * k_k[g] for g in streams]
        n2 = segsum([x * x for x in kk])
        kk = [kk[g] / jnp.maximum(jnp.sqrt(n2[g]), 1e-12) for g in streams]
        k2 = [k[g] * (1.0 + (a[g] - 1.0) * k_a[g]) for g in streams]
        b_s = [kk[g] * a[g] for g in streams]

        cs = [jnp.dot(tri, jnp.concatenate(_split2(lw[g]), axis=1), preferred_element_type=F32)
              for g in streams]
        cum = [c[:, :L] + c[:, L:] for c in cs]
        total = [c[C - 1:C, :] for c in cum]
        at = [-kk[g] * jnp.exp(cum[g] - lw[g]) for g in streams]
        rt = [r[g] * jnp.exp(cum[g]) for g in streams]
        w_inv = [jnp.exp(-c) for c in cum]
        w_tail = [jnp.exp(total[g] - cum[g]) for g in streams]

        aa = [_dot_t(jnp.concatenate([at[g], rt[g]], axis=0).astype(BF16),
                     jnp.concatenate([blockstack(b_s[g] * w_inv[g]),
                                      blockstack(k2[g] * w_inv[g])], axis=0))
              for g in streams]
        a_ab = [jnp.where(strict_lower, x[:C, :hc], 0.0) for x in aa]
        a_ak = [jnp.where(strict_lower, x[:C, hc:], 0.0) for x in aa]
        a_rr = [jnp.concatenate([jnp.where(lower, x[C:, :hc], 0.0),
                                 jnp.where(lower, x[C:, hc:], 0.0)], axis=1) for x in aa]

        t_inv = [eye + x for x in a_ab]
        p = [mm(x, blockstack(x)) for x in a_ab]
        for i in range(SCAN_DOUBLINGS):
            bd = [blockstack(x) for x in p]
            if i + 1 < SCAN_DOUBLINGS:
                pt = [mm(jnp.concatenate([p[g], t_inv[g]], axis=0), bd[g]) for g in streams]
                p = [x[:C] for x in pt]
                t_inv = [t_inv[g] + pt[g][C:] for g in streams]
            else:
                t_inv = [t_inv[g] + mm(t_inv[g], bd[g]) for g in streams]

        state = [state_ref[g] for g in streams]
        st = [x.astype(BF16) for x in state]
        bs_v = [blockstack(x) for x in v]
        z = [_dot_t(at[g].astype(BF16), st[g]) + mm(a_ak[g], bs_v[g]) for g in streams]
        u = [mm(t_inv[g], blockstack(z[g])) for g in streams]
        y = [_dot_t(rt[g].astype(BF16), st[g])
             + mm(a_rr[g], jnp.concatenate([blockstack(u[g]), bs_v[g]], axis=0))
             for g in streams]
        upd = [_dot_tl(jnp.concatenate([u[g], v[g]], axis=0).astype(BF16),
                       jnp.concatenate([b_s[g] * w_tail[g], k2[g] * w_tail[g]],
                                       axis=0).astype(BF16)) for g in streams]
        for g in streams:
            state_ref[g] = state[g] * jnp.exp(total[g]) + jnp.where(bd_mask, upd[g], 0.0)

        mean = segsum(y)
        yc = [y[g] - mean[g] * inv_n for g in streams]
        var = segsum([x * x for x in yc])
        bonus = segsum([r[g] * k2[g] * r_k[g] for g in streams])
        for g in streams:
            yn = (yc[g] * lax.rsqrt(var[g] * inv_n + RWKV_LN_EPS)) * ln_w[g] + ln_b[g]
            out = (yn + bonus[g] * v[g]) * (gate[g] * jax.nn.sigmoid(gate[g]))
            o_ref[sl, lanes(g)] = out.astype(o_ref.dtype)
        return carry

    lax.fori_loop(0, n_chunks, chunk, 0)


def _rwkv_scan(proj, v, v_idx, lw, a, k_k, k_a, r_k, ln_w, ln_b, B, S, ts=256, n_streams=8):
    _, M, D = proj.shape
    ts = _tile(S, ts)
    L = SCAN_LANES
    W = n_streams * L
    nt = S // ts
    pj = lambda j: pl.BlockSpec((None, ts, W), lambda b, h, c, j=j: (j, b * nt + c, h))
    tile = pl.BlockSpec((ts, W), lambda b, h, c: (b * nt + c, h))
    par = pl.BlockSpec((1, W), lambda b, h, c: (0, h))
    vec = lambda p: p.reshape(1, D)
    return pl.pallas_call(
        functools.partial(_scan_body, n_chunks=ts // SCAN_CHUNK),
        grid=(B, D // W, nt),
        in_specs=[pj(0), pj(1), pj(v_idx), tile, tile, pj(3), par, par, par, par, par],
        out_specs=tile,
        out_shape=jax.ShapeDtypeStruct((M, D), BF16),
        scratch_shapes=[pltpu.VMEM((n_streams, L, L), F32)],
        compiler_params=_params(("parallel", "parallel", "arbitrary")),
        name="rwkv_scan",
    )(proj, proj, v, lw, a, proj, vec(k_k), vec(k_a), vec(r_k), vec(ln_w), vec(ln_b))


def _rwkv_layer(x, nw, mu, w_in, w0, w1, w2, a0, a1, a2, k_k, k_a, r_k, ln_w, ln_b, w_out,
                v_first, vres):
    B, S, D = x.shape
    M = B * S
    mix = _rwkv_prep(x, nw, mu).reshape(6, M, D)
    proj = _matmul(mix, w_in.astype(BF16))
    outs = _rwkv_lora(mix, proj, v_first, w0, w1, w2, a0, a1, a2, vres)
    if vres is None:
        lw, a = outs
        v, v_idx = proj, 2
        v_first = proj
    else:
        lw, a, v = outs
        v, v_idx = v[None], 0
    yg = _rwkv_scan(proj, v, v_idx, lw, a, k_k, k_a, r_k, ln_w, ln_b, B, S)
    x2 = _matmul(yg[None], w_out.astype(BF16)[None], res=x.reshape(1, M, D))
    return x2.reshape(B, S, D), v_first


def _mm_cols_body(a_ref, w_ref, o_ref):
    acc = jnp.dot(a_ref[...], w_ref[...], preferred_element_type=F32)
    for c in range(o_ref.shape[0]):
        o_ref[c] = acc[:, c * V7X_LANES:(c + 1) * V7X_LANES]


def _matmul_cols(a, w, tm=1024, tn=1024):
    M, K = a.shape
    N = w.shape[-1]
    tm = _tile(M, tm)
    tn = _tile(N, tn)
    nc = tn // V7X_LANES
    return pl.pallas_call(
        _mm_cols_body,
        grid=(M // tm, N // tn),
        in_specs=[pl.BlockSpec((tm, K), lambda i, j: (i, 0)),
                  pl.BlockSpec((K, tn), lambda i, j: (0, j))],
        out_specs=pl.BlockSpec((nc, tm, V7X_LANES), lambda i, j: (j, i, 0)),
        out_shape=jax.ShapeDtypeStruct((N // V7X_LANES, M, V7X_LANES), F32),
        compiler_params=_params(("parallel", "arbitrary")),
        name="matmul_cols",
    )(a, w)


def _rope_body(pos_ref, invf_ref, cos_ref, slo_ref, shi_ref):
    half = ROPE_DIM // 2
    ang = pos_ref[...].astype(F32) * invf_ref[...]
    sin = jnp.sin(ang)
    lane = lax.broadcasted_iota(jnp.int32, ang.shape, 1)
    cos_ref[...] = jnp.cos(ang)
    slo_ref[...] = jnp.where(lane < half, -sin, 0.0)
    shi_ref[...] = jnp.where((lane >= half) & (lane < ROPE_DIM), sin, 0.0)


def _rope_tables(positions, ts=512):
    B, S = positions.shape
    Dh = ATT_HEAD_DIM
    ts = _tile(S, ts)
    inv_freq = ROPE_THETA ** (-jnp.arange(0, ROPE_DIM, 2, dtype=F32) / ROPE_DIM)
    invf = jnp.zeros((Dh,), F32).at[:ROPE_DIM].set(jnp.tile(inv_freq, 2)).reshape(1, Dh)
    out = pl.BlockSpec((None, ts, Dh), lambda b, s: (b, s, 0))
    shp = jax.ShapeDtypeStruct((B, S, Dh), F32)
    return pl.pallas_call(
        _rope_body,
        grid=(B, S // ts),
        in_specs=[pl.BlockSpec((None, ts, 1), lambda b, s: (b, s, 0)),
                  pl.BlockSpec((1, Dh), lambda b, s: (0, 0))],
        out_specs=[out] * 3,
        out_shape=[shp] * 3,
        compiler_params=_params(("parallel", "arbitrary")),
        name="rope_tables",
    )(positions.reshape(B, S, 1), invf)


def _attn_body(cos_ref, slo_ref, shi_ref, qn_ref, kn_ref, *refs):
    G = len(DILATED_CONFIG)
    q_refs, k_refs, v_refs = refs[:G], refs[G:2 * G], refs[2 * G:3 * G]
    gate_ref, out_ref, qs_ref, ks_ref = refs[3 * G:3 * G + 4]
    os_refs, ls_refs = refs[3 * G + 4:4 * G + 4], refs[4 * G + 4:]
    S, Dh = qs_ref.shape
    BLK = ATT_BLOCK
    half = ROPE_DIM // 2
    scale = Dh ** -0.5
    rt = 256

    def norm_rope(x, w, rows):
        xn = _rms(x, w)
        return (xn * cos_ref[rows, :] + pltpu.roll(xn, Dh - half, axis=1) * slo_ref[rows, :]
                + pltpu.roll(xn, half, axis=1) * shi_ref[rows, :])

    def band_mask(kw, off):
        qi = lax.broadcasted_iota(jnp.int32, (BLK, kw), 0)
        kj = lax.broadcasted_iota(jnp.int32, (BLK, kw), 1)
        return (kj >= qi + (off - BLK)) & (kj <= qi + off)

    first_mask = band_mask(BLK, 0)
    later_mask = band_mask(2 * BLK, BLK)

    for g, (_, d) in enumerate(DILATED_CONFIG):
        for t in range(S // rt):
            rows = slice(t * rt, (t + 1) * rt)
            qs_ref[rows, :] = norm_rope(q_refs[g][rows, :], qn_ref[g], rows)
            ks_ref[rows, :] = norm_rope(k_refs[g][rows, :], kn_ref[g], rows)
        L = S // d
        for rho in range(d):
            for n in range(L // BLK):
                sub = lambda blk, size: pl.ds(rho + blk * BLK * d, size, stride=d)
                q_rows = sub(n, BLK)
                if n == 0:
                    k_rows, mask = sub(0, BLK), first_mask
                else:
                    k_rows, mask = sub(n - 1, 2 * BLK), later_mask
                s = _dot_t(qs_ref[q_rows, :].astype(BF16), ks_ref[k_rows, :].astype(BF16)) * scale
                s = jnp.where(mask, s, -jnp.inf)
                m = jnp.max(s, axis=-1, keepdims=True)
                p = jnp.exp(s - m)
                l = jnp.sum(p, axis=-1, keepdims=True)
                o = jnp.dot(p.astype(BF16), v_refs[g][k_rows, :].astype(BF16),
                            preferred_element_type=F32) / l
                os_refs[g][q_rows, :] = o
                ls_refs[g][q_rows, :] = jnp.broadcast_to(m + jnp.log(l), (BLK, Dh))

    for t in range(S // rt):
        rows = slice(t * rt, (t + 1) * rt)
        lse = [ls_refs[g][rows, :] for g in range(G)]
        m = functools.reduce(jnp.maximum, lse)
        e = [jnp.exp(x - m) for x in lse]
        den = functools.reduce(lambda a, b: a + b, e)
        o = functools.reduce(lambda a, b: a + b,
                             [(e[g] / den) * os_refs[g][rows, :] for g in range(G)])
        gt = gate_ref[rows, :]
        out_ref[rows, :] = (o * (gt * jax.nn.sigmoid(gt))).astype(out_ref.dtype)


def _attn_mix(proj, tables, qn_w, kn_w, B, S):
    G = len(DILATED_CONFIG)
    H = ATT_HEADS
    Dh = ATT_HEAD_DIM
    col = lambda c0: pl.BlockSpec((None, S, Dh), lambda b, h, c0=c0: (c0 + h, b, 0))
    tab = pl.BlockSpec((None, S, Dh), lambda b, h: (b, 0, 0))
    nrm = pl.BlockSpec((G, 1, Dh), lambda b, h: (0, 0, 0))
    return pl.pallas_call(
        _attn_body,
        grid=(B, H),
        in_specs=[tab, tab, tab, nrm, nrm] + [col(c * H) for c in range(3 * G + 1)],
        out_specs=pl.BlockSpec((S, Dh), lambda b, h: (b, h)),
        out_shape=jax.ShapeDtypeStruct((B * S, H * Dh), BF16),
        scratch_shapes=[pltpu.VMEM((S, Dh), F32)] * (2 + 2 * G),
        compiler_params=_params(("parallel", "arbitrary")),
        name="dilated_attn",
    )(*tables, qn_w.reshape(G, 1, Dh), kn_w.reshape(G, 1, Dh), *([proj] * (3 * G + 1)))


def _attn_layer(x, positions, nw, w_in, qn_w, kn_w, w_out):
    B, S, D = x.shape
    M = B * S
    h = _norm(x.reshape(M, D), nw)
    proj = _matmul_cols(h, w_in.astype(BF16))
    og = _attn_mix(proj, _rope_tables(positions), qn_w, kn_w, B, S)
    x2 = _matmul(og[None], w_out.astype(BF16)[None], res=x.reshape(1, M, D))
    return x2.reshape(B, S, D)


def _pool_body(u_ref, up_ref, g_ref, wg_ref, sc_ref, o_ref):
    s = pl.program_id(1)
    ts = u_ref.shape[0]
    Wg = wg_ref.shape[-1]
    t = s * ts + lax.broadcasted_iota(jnp.int32, (ts, 1), 0)
    for j, win in enumerate(POOL_WINDOWS):
        cols = slice(j * Wg, (j + 1) * Wg)
        u = u_ref[:, cols]
        halo = jnp.where(s == 0, 0.0, up_ref[:, cols])
        acc = jnp.concatenate([halo, u], axis=0)
        step = 1
        while step < win:
            acc = acc + pltpu.roll(acc, step, axis=0)
            step *= 2
        cnt = jnp.minimum(t + 1, win).astype(F32)
        diff = acc[POOL_HALO:, :] / cnt - u
        y = jnp.dot(diff.astype(BF16), wg_ref[j], preferred_element_type=F32) * sc_ref[:, cols]
        g = g_ref[:, cols]
        o_ref[:, cols] = (y * (g * jax.nn.sigmoid(g))).astype(o_ref.dtype)


def _pool_mix(proj, w_grp, scale, B, S, ts=256):
    W = proj.shape[-1] // 2
    ts = _tile(S, ts)
    hb = ts // POOL_HALO
    return pl.pallas_call(
        _pool_body,
        grid=(B, S // ts),
        in_specs=[pl.BlockSpec((None, ts, W), lambda b, s: (b, s, 0)),
                  pl.BlockSpec((None, POOL_HALO, W),
                               lambda b, s: (b, jnp.maximum(s * hb - 1, 0), 0)),
                  pl.BlockSpec((None, ts, W), lambda b, s: (b, s, 1)),
                  pl.BlockSpec(w_grp.shape, lambda b, s: (0, 0, 0)),
                  pl.BlockSpec((1, W), lambda b, s: (0, 0))],
        out_specs=pl.BlockSpec((None, ts, W), lambda b, s: (b, s, 0)),
        out_shape=jax.ShapeDtypeStruct((B, S, W), BF16),
        compiler_params=_params(("parallel", "arbitrary")),
        name="pool_mix",
    )(proj, proj, proj, w_grp.astype(BF16), scale.reshape(1, W))


def _pool_layer(x, nw, w_in, w_grp, scale, w_out):
    B, S, D = x.shape
    M = B * S
    h = _norm(x.reshape(M, D), nw)
    proj = _matmul(h[None], w_in.astype(BF16)[None])[0]
    yg = _pool_mix(proj.reshape(B, S, -1), w_grp, scale, B, S)
    x2 = _matmul(yg.reshape(1, M, -1), w_out.astype(BF16)[None], res=x.reshape(1, M, D))
    return x2.reshape(B, S, D)


def kernel(x, positions, norm_w, a_mu, a_w_in, a_w0, a_w1, a_w2, a_a0, a_a1, a_a2, a_k_k, a_k_a,
           a_r_k, a_lnx_w, a_lnx_b, a_v0, a_v1, a_v2, a_w_out, b_w_in, b_qn_w, b_kn_w, b_w_out,
           c_w_in, c_w_grp, c_scale, c_w_out):
    depth = norm_w.shape[0]
    ia = ib = ic = 0
    v_first = None
    for i in range(depth):
        kind = i % 3
        if kind == 0:
            vres = None if ia == 0 else (a_v0[ia - 1], a_v1[ia - 1], a_v2[ia - 1])
            x, v_first = _rwkv_layer(
                x, norm_w[i], a_mu[ia], a_w_in[ia], a_w0[ia], a_w1[ia], a_w2[ia], a_a0[ia],
                a_a1[ia], a_a2[ia], a_k_k[ia], a_k_a[ia], a_r_k[ia].reshape(-1), a_lnx_w[ia],
                a_lnx_b[ia], a_w_out[ia], v_first, vres)
            ia += 1
        elif kind == 1:
            x = _attn_layer(x, positions, norm_w[i], b_w_in[ib], b_qn_w[ib], b_kn_w[ib],
                            b_w_out[ib])
            ib += 1
        else:
            x = _pool_layer(x, norm_w[i], c_w_in[ic], c_w_grp[ic], c_scale[ic], c_w_out[ic])
            ic += 1
    return x
```

```python
import functools

import jax
import jax.numpy as jnp
from jax import lax
from jax.experimental import pallas as pl
from jax.experimental.pallas import tpu as pltpu

F32 = jnp.float32
BF16 = jnp.bfloat16

RMS_EPS = 1e-6
RWKV_HEAD_DIM = 64
RWKV_LN_EPS = 64e-5
DECAY_SCALE = 0.6065306597126334
ATT_HEAD_DIM = 128
ATT_HEADS = 8
ATT_BRANCH = ATT_HEADS * ATT_HEAD_DIM
DILATED_CONFIG = ((128, 1), (512, 4), (2048, 16))
ATT_BLOCK = 128
ATT_TILE_BATCH = 8
ROPE_THETA = 500000.0
ROPE_DIM = ATT_HEAD_DIM // 4
POOL_WINDOWS = (2, 4, 8, 16)
POOL_HALO = 16

V7X_LANES = 128
V7X_MXU_DIM = 256
V7X_VMEM_BYTES = 64 * 1024 * 1024
VMEM_LIMIT = 48 * 1024 * 1024

PROLOGUE_ROWS = 256
SCAN_CHUNK = 64
SCAN_LANES = V7X_MXU_DIM
SCAN_HEADS = SCAN_LANES // RWKV_HEAD_DIM
SCAN_DOUBLINGS = 5


def _params(sem):
    return pltpu.CompilerParams(dimension_semantics=sem, vmem_limit_bytes=VMEM_LIMIT)


def _tile(n, pref):
    t = min(n, pref)
    while n % t:
        t //= 2
    return t


def _rms(x, w):
    return (x * lax.rsqrt(jnp.mean(x * x, axis=-1, keepdims=True) + RMS_EPS)) * w


def _in_proj_body(*refs, cols):
    if len(refs) == 3:
        lhs_ref, w_ref, o_ref = refs
    else:
        x_ref, nw_ref, w_ref, o_ref, lhs_ref = refs
        tm = x_ref.shape[0]

        @pl.when(pl.program_id(2) == 0)
        def _():
            nw = nw_ref[...]
            for r0 in range(0, tm, PROLOGUE_ROWS):
                rows = slice(r0, r0 + PROLOGUE_ROWS)
                lhs_ref[rows, :] = _rms(x_ref[rows, :], nw).astype(BF16)

    acc = jnp.dot(lhs_ref[...], w_ref[...], preferred_element_type=F32)
    if cols:
        for c in range(o_ref.shape[0]):
            o_ref[c] = acc[:, c * V7X_LANES:(c + 1) * V7X_LANES]
    else:
        o_ref[...] = acc


def _in_proj(x, nw, w, cols=False, tm=1024, tn=1024):
    M, K = x.shape[-2:]
    G, _, N = w.shape
    tm = _tile(M, tm)
    tn = _tile(N, tn)
    w_spec = pl.BlockSpec((None, K, tn), lambda g, i, j: (g, 0, j))
    if nw is None:
        in_specs = [pl.BlockSpec((None, tm, K), lambda g, i, j: (g, i, 0)), w_spec]
        args = [x, w]
        scratch = []
    else:
        in_specs = [pl.BlockSpec((tm, K), lambda g, i, j: (i, 0)),
                    pl.BlockSpec((1, K), lambda g, i, j: (0, 0)), w_spec]
        args = [x, nw.reshape(1, K), w]
        scratch = [pltpu.VMEM((tm, K), BF16)]
    if cols:
        nc = tn // V7X_LANES
        out_spec = pl.BlockSpec((nc, tm, V7X_LANES), lambda g, i, j: (j, i, 0))
        out_shape = jax.ShapeDtypeStruct((N // V7X_LANES, M, V7X_LANES), F32)
    else:
        out_spec = pl.BlockSpec((None, tm, tn), lambda g, i, j: (g, i, j))
        out_shape = jax.ShapeDtypeStruct((G, M, N), F32)
    return pl.pallas_call(
        functools.partial(_in_proj_body, cols=cols),
        grid=(G, M // tm, N // tn),
        in_specs=in_specs,
        out_specs=out_spec,
        out_shape=out_shape,
        scratch_shapes=scratch,
        compiler_params=_params(("parallel", "parallel", "arbitrary")),
        name="in_proj",
    )(*args)


def _out_proj_body(a_ref, w_ref, r_ref, o_ref):
    o_ref[...] = r_ref[...] + jnp.dot(a_ref[...], w_ref[...], preferred_element_type=F32)


def _out_proj(a, w, res, tm=1024, tn=1024):
    M, K = a.shape
    N = w.shape[-1]
    tm = _tile(M, tm)
    tn = _tile(N, tn)
    tile = pl.BlockSpec((tm, tn), lambda i, j: (i, j))
    return pl.pallas_call(
        _out_proj_body,
        grid=(M // tm, N // tn),
        in_specs=[pl.BlockSpec((tm, K), lambda i, j: (i, 0)),
                  pl.BlockSpec((K, tn), lambda i, j: (0, j)), tile],
        out_specs=tile,
        out_shape=jax.ShapeDtypeStruct((M, N), F32),
        compiler_params=_params(("parallel", "arbitrary")),
        name="out_proj",
    )(a, w, res)


def _rwkv_prep_body(x_ref, xp_ref, nw_ref, mu_ref, o_ref):
    s = pl.program_id(1)
    nw = nw_ref[...]
    h = _rms(x_ref[...], nw)
    hp = _rms(xp_ref[...], nw)[7:8, :]
    hp = jnp.where(s == 0, 0.0, hp)
    row = lax.broadcasted_iota(jnp.int32, h.shape, 0)
    hs = jnp.where(row == 0, hp, pltpu.roll(h, 1, axis=0))
    xx = hs - h
    for j in range(6):
        o_ref[j] = (h + xx * mu_ref[j:j + 1, :]).astype(o_ref.dtype)


def _rwkv_prep(x, nw, mu, ts=256):
    B, S, D = x.shape
    ts = _tile(S, ts)
    sub = 8
    return pl.pallas_call(
        _rwkv_prep_body,
        grid=(B, S // ts),
        in_specs=[pl.BlockSpec((None, ts, D), lambda b, s: (b, s, 0)),
                  pl.BlockSpec((None, sub, D),
                               lambda b, s: (b, jnp.maximum(s * (ts // sub) - 1, 0), 0)),
                  pl.BlockSpec((1, D), lambda b, s: (0, 0)),
                  pl.BlockSpec((6, D), lambda b, s: (0, 0))],
        out_specs=pl.BlockSpec((6, None, ts, D), lambda b, s: (0, b, s, 0)),
        out_shape=jax.ShapeDtypeStruct((6, B, S, D), BF16),
        compiler_params=_params(("parallel", "arbitrary")),
        name="rwkv_prep",
    )(x, x, nw.reshape(1, D), mu)


def _rwkv_lora_body(*refs, has_v):
    if has_v:
        (xw_ref, xa_ref, xv_ref, v_ref, vf_ref,
         w1_ref, w2_ref, w0_ref, a1_ref, a2_ref, a0_ref, v1_ref, v2_ref, v0_ref,
         olw_ref, oa_ref, ov_ref) = refs
    else:
        (xw_ref, xa_ref, w1_ref, w2_ref, w0_ref, a1_ref, a2_ref, a0_ref,
         olw_ref, oa_ref) = refs
    t = jnp.tanh(jnp.dot(xw_ref[...], w1_ref[...], preferred_element_type=F32))
    z = w0_ref[...] + jnp.dot(t.astype(BF16), w2_ref[...], preferred_element_type=F32)
    olw_ref[...] = -DECAY_SCALE * jax.nn.sigmoid(z)
    ta = jnp.dot(xa_ref[...], a1_ref[...], preferred_element_type=F32)
    oa_ref[...] = jax.nn.sigmoid(
        a0_ref[...] + jnp.dot(ta.astype(BF16), a2_ref[...], preferred_element_type=F32))
    if has_v:
        tv = jnp.dot(xv_ref[...], v1_ref[...], preferred_element_type=F32)
        sv = jax.nn.sigmoid(
            v0_ref[...] + jnp.dot(tv.astype(BF16), v2_ref[...], preferred_element_type=F32))
        v = v_ref[...]
        ov_ref[...] = v + (vf_ref[...] - v) * sv


def _pad_lora(w1, w2):
    r = w1.shape[1]
    rp = -(-r // V7X_LANES) * V7X_LANES
    return (jnp.pad(w1, ((0, 0), (0, rp - r))).astype(BF16),
            jnp.pad(w2, ((0, rp - r), (0, 0))).astype(BF16))


def _rwkv_lora(mix, proj, v_first, w0, w1, w2, a0, a1, a2, vres, tm=256):
    _, M, D = mix.shape
    tm = _tile(M, tm)
    has_v = vres is not None
    row = lambda j: pl.BlockSpec((None, tm, D), lambda i, j=j: (j, i, 0))
    full = lambda a: pl.BlockSpec(a.shape, lambda i: (0,) * a.ndim)
    tile = pl.BlockSpec((tm, D), lambda i: (i, 0))
    w1p, w2p = _pad_lora(w1, w2)
    a1p, a2p = _pad_lora(a1, a2)
    weights = [w1p, w2p, w0.reshape(1, D), a1p, a2p, a0.reshape(1, D)]
    args = [mix, mix]
    in_specs = [row(4), row(5)]
    n_out = 2
    if has_v:
        v0, v1, v2 = vres
        v1p, v2p = _pad_lora(v1, v2)
        weights += [v1p, v2p, v0.reshape(1, D)]
        args += [mix, proj, v_first]
        in_specs += [row(2), row(2), row(2)]
        n_out = 3
    args += weights
    in_specs += [full(w) for w in weights]
    return pl.pallas_call(
        functools.partial(_rwkv_lora_body, has_v=has_v),
        grid=(M // tm,),
        in_specs=in_specs,
        out_specs=[tile] * n_out,
        out_shape=[jax.ShapeDtypeStruct((M, D), F32)] * n_out,
        compiler_params=_params(("parallel",)),
        name="rwkv_lora",
    )(*args)


def _split2(x):
    hi = x.astype(BF16)
    return hi, (x - hi.astype(F32)).astype(BF16)


def _dot_t(a, b):
    return lax.dot_general(a, b, (((1,), (1,)), ((), ())), preferred_element_type=F32)


def _dot_tl(a, b):
    return lax.dot_general(a, b, (((0,), (0,)), ((), ())), preferred_element_type=F32)


def _scan_body(r_ref, k_ref, v_ref, lw_ref, a_ref, g_ref,
               kk_ref, ka_ref, rk_ref, lnw_ref, lnb_ref, o_ref, state_ref, *, n_chunks):
    C, L, H, N = SCAN_CHUNK, SCAN_LANES, SCAN_HEADS, RWKV_HEAD_DIM
    n_rows, _, D = r_ref.shape
    ids = [(b, g) for b in range(n_rows) for g in range(D // L)]
    streams = range(len(ids))

    @pl.when(pl.program_id(1) == 0)
    def _():
        state_ref[...] = jnp.zeros_like(state_ref)

    lane_head = lax.broadcasted_iota(jnp.int32, (C, L), 1) // N
    head_masks = [lane_head == h for h in range(H)]
    t_idx = lax.broadcasted_iota(jnp.int32, (C, H * C), 0)
    s_idx = lax.broadcasted_iota(jnp.int32, (C, H * C), 1) % C
    strict_lower = s_idx < t_idx
    lower = s_idx <= t_idx
    eye = (s_idx == t_idx).astype(F32)
    bd_mask = (lax.broadcasted_iota(jnp.int32, (L, L), 0) // N
               == lax.broadcasted_iota(jnp.int32, (L, L), 1) // N)
    ones_bd = bd_mask.astype(BF16)
    tri = (lax.broadcasted_iota(jnp.int32, (C, C), 1)
           <= lax.broadcasted_iota(jnp.int32, (C, C), 0)).astype(BF16)
    hc = H * C
    inv_n = 1.0 / N

    def blockstack(x):
        return jnp.concatenate([jnp.where(m, x, 0.0) for m in head_masks], axis=0).astype(BF16)

    def segsum(xs):
        s = jnp.dot(jnp.concatenate(xs, axis=0).astype(BF16), ones_bd, preferred_element_type=F32)
        return [s[g * C:(g + 1) * C] for g in range(len(xs))]

    def mm(a, b):
        return jnp.dot(a.astype(BF16), b, preferred_element_type=F32)

    lanes = lambda g: slice(g * L, (g + 1) * L)
    split = lambda ref: [ref[:, lanes(g)] for _, g in ids]
    k_k, k_a, r_k, ln_w, ln_b = (split(ref) for ref in (kk_ref, ka_ref, rk_ref, lnw_ref, lnb_ref))

    def chunk(ci, carry):
        sl = pl.ds(pl.multiple_of(ci * C, C), C)
        load = lambda ref: [ref[b, sl, lanes(g)] for b, g in ids]
        r, k, v, lw, a, gate = (load(ref) for ref in (r_ref, k_ref, v_ref, lw_ref, a_ref, g_ref))

        kk = [k[g] * k_k[g] for g in streams]
        n2 = segsum([x * x for x in kk])
        kk = [kk[g] / jnp.maximum(jnp.sqrt(n2[g]), 1e-12) for g in streams]
        k2 = [k[g] * (1.0 + (a[g] - 1.0) * k_a[g]) for g in streams]
        b_s = [kk[g] * a[g] for g in streams]

        cs = [jnp.dot(tri, jnp.concatenate(_split2(lw[g]), axis=1), preferred_element_type=F32)
              for g in streams]
        cum = [c[:, :L] + c[:, L:] for c in cs]
        total = [c[C - 1:C, :] for c in cum]
        at = [-kk[g] * jnp.exp(cum[g] - lw[g]) for g in streams]
        rt = [r[g] * jnp.exp(cum[g]) for g in streams]
        w_inv = [jnp.exp(-c) for c in cum]
        w_tail = [jnp.exp(total[g] - cum[g]) for g in streams]

        aa = [_dot_t(jnp.concatenate([at[g], rt[g]], axis=0).astype(BF16),
                     jnp.concatenate([blockstack(b_s[g] * w_inv[g]),
                                      blockstack(k2[g] * w_inv[g])], axis=0))
              for g in streams]
        a_ab = [jnp.where(strict_lower, x[:C, :hc], 0.0) for x in aa]
        a_ak = [jnp.where(strict_lower, x[:C, hc:], 0.0) for x in aa]
        a_rr = [jnp.concatenate([jnp.where(lower, x[C:, :hc], 0.0),
                                 jnp.where(lower, x[C:, hc:], 0.0)], axis=1) for x in aa]

        t_inv = [eye + x for x in a_ab]
        p = [mm(x, blockstack(x)) for x in a_ab]
        for i in range(SCAN_DOUBLINGS):
            bd = [blockstack(x) for x in p]
            if i + 1 < SCAN_DOUBLINGS:
                pt = [mm(jnp.concatenate([p[g], t_inv[g]], axis=0), bd[g]) for g in streams]
                p = [x[:C] for x in pt]
                t_inv = [t_inv[g] + pt[g][C:] for g in streams]
            else:
                t_inv = [t_inv[g] + mm(t_inv[g], bd[g]) for g in streams]

        state = [state_ref[g] for g in streams]
        st = [x.astype(BF16) for x in state]
        bs_v = [blockstack(x) for x in v]
        z = [_dot_t(at[g].astype(BF16), st[g]) + mm(a_ak[g], bs_v[g]) for g in streams]
        u = [mm(t_inv[g], blockstack(z[g])) for g in streams]
        y = [_dot_t(rt[g].astype(BF16), st[g])
             + mm(a_rr[g], jnp.concatenate([blockstack(u[g]), bs_v[g]], axis=0))
             for g in streams]
        upd = [_dot_tl(jnp.concatenate([u[g], v[g]], axis=0).astype(BF16),
                       jnp.concatenate([b_s[g] * w_tail[g], k2[g] * w_tail[g]],
                                       axis=0).astype(BF16)) for g in streams]
        for g in streams:
            state_ref[g] = state[g] * jnp.exp(total[g]) + jnp.where(bd_mask, upd[g], 0.0)

        mean = segsum(y)
        yc = [y[g] - mean[g] * inv_n for g in streams]
        var = segsum([x * x for x in yc])
        bonus = segsum([r[g] * k2[g] * r_k[g] for g in streams])
        for g in streams:
            yn = (yc[g] * lax.rsqrt(var[g] * inv_n + RWKV_LN_EPS)) * ln_w[g] + ln_b[g]
            out = (yn + bonus[g] * v[g]) * (gate[g] * jax.nn.sigmoid(gate[g]))
            o_ref[ids[g][0], sl, lanes(ids[g][1])] = out.astype(o_ref.dtype)
        return carry

    lax.fori_loop(0, n_chunks, chunk, 0)


def _rwkv_scan(proj, v, v_idx, lw, a, k_k, k_a, r_k, ln_w, ln_b, B, S, ts=128, n_rows=2):
    _, M, D = proj.shape
    ts = _tile(S, ts)
    n_rows = _tile(B, n_rows)
    L = SCAN_LANES
    pj = lambda j: pl.BlockSpec((None, n_rows, ts, D), lambda b, c, j=j: (j, b, c, 0))
    tile = pl.BlockSpec((n_rows, ts, D), lambda b, c: (b, c, 0))
    par = pl.BlockSpec((1, D), lambda b, c: (0, 0))
    vec = lambda p: p.reshape(1, D)
    seq = lambda t: t.reshape(t.shape[:-2] + (B, S, D))
    return pl.pallas_call(
        functools.partial(_scan_body, n_chunks=ts // SCAN_CHUNK),
        grid=(B // n_rows, S // ts),
        in_specs=[pj(0), pj(1), pj(v_idx), tile, tile, pj(3), par, par, par, par, par],
        out_specs=tile,
        out_shape=jax.ShapeDtypeStruct((B, S, D), BF16),
        scratch_shapes=[pltpu.VMEM((n_rows * (D // L), L, L), F32)],
        compiler_params=_params(("parallel", "arbitrary")),
        name="rwkv_scan",
    )(seq(proj), seq(proj), seq(v), seq(lw), seq(a), seq(proj),
      vec(k_k), vec(k_a), vec(r_k), vec(ln_w), vec(ln_b)).reshape(M, D)


def _rwkv_layer(x, nw, mu, w_in, w0, w1, w2, a0, a1, a2, k_k, k_a, r_k, ln_w, ln_b, w_out,
                v_first, vres):
    B, S, D = x.shape
    M = B * S
    mix = _rwkv_prep(x, nw, mu).reshape(6, M, D)
    proj = _in_proj(mix, None, w_in.astype(BF16))
    outs = _rwkv_lora(mix, proj, v_first, w0, w1, w2, a0, a1, a2, vres)
    if vres is None:
        lw, a = outs
        v, v_idx = proj, 2
        v_first = proj
    else:
        lw, a, v = outs
        v, v_idx = v[None], 0
    yg = _rwkv_scan(proj, v, v_idx, lw, a, k_k, k_a, r_k, ln_w, ln_b, B, S)
    return _out_proj(yg, w_out.astype(BF16), x.reshape(M, D)).reshape(B, S, D), v_first


def _rope_body(pos_ref, invf_ref, cos_ref, slo_ref, shi_ref):
    half = ROPE_DIM // 2
    ang = pos_ref[...].astype(F32) * invf_ref[...]
    sin = jnp.sin(ang)
    lane = lax.broadcasted_iota(jnp.int32, ang.shape, 1)
    cos_ref[...] = jnp.cos(ang)
    slo_ref[...] = jnp.where(lane < half, -sin, 0.0)
    shi_ref[...] = jnp.where((lane >= half) & (lane < ROPE_DIM), sin, 0.0)


def _rope_tables(positions, ts=512):
    B, S = positions.shape
    Dh = ATT_HEAD_DIM
    ts = _tile(S, ts)
    inv_freq = ROPE_THETA ** (-jnp.arange(0, ROPE_DIM, 2, dtype=F32) / ROPE_DIM)
    invf = jnp.zeros((Dh,), F32).at[:ROPE_DIM].set(jnp.tile(inv_freq, 2)).reshape(1, Dh)
    out = pl.BlockSpec((None, ts, Dh), lambda b, s: (b, s, 0))
    shp = jax.ShapeDtypeStruct((B, S, Dh), F32)
    return pl.pallas_call(
        _rope_body,
        grid=(B, S // ts),
        in_specs=[pl.BlockSpec((None, ts, 1), lambda b, s: (b, s, 0)),
                  pl.BlockSpec((1, Dh), lambda b, s: (0, 0))],
        out_specs=[out] * 3,
        out_shape=[shp] * 3,
        compiler_params=_params(("parallel", "arbitrary")),
        name="rope_tables",
    )(positions.reshape(B, S, 1), invf)


def _attn_body(cos_ref, slo_ref, shi_ref, qn_ref, kn_ref, *refs):
    G = len(DILATED_CONFIG)
    q_refs, k_refs, v_refs = refs[:G], refs[G:2 * G], refs[2 * G:3 * G]
    gate_ref, out_ref, qs_ref, ks_ref = refs[3 * G:3 * G + 4]
    os_refs, ls_refs = refs[3 * G + 4:4 * G + 4], refs[4 * G + 4:]
    S, Dh = qs_ref.shape
    BLK = ATT_BLOCK
    half = ROPE_DIM // 2
    scale = Dh ** -0.5
    rt = min(S, 2048)

    def norm_rope(x, w, rows):
        xn = _rms(x, w)
        return (xn * cos_ref[rows, :] + pltpu.roll(xn, Dh - half, axis=1) * slo_ref[rows, :]
                + pltpu.roll(xn, half, axis=1) * shi_ref[rows, :])

    def band_mask(kw, off):
        qi = lax.broadcasted_iota(jnp.int32, (BLK, kw), 0)
        kj = lax.broadcasted_iota(jnp.int32, (BLK, kw), 1)
        return (kj >= qi + (off - BLK)) & (kj <= qi + off)

    first_mask = band_mask(BLK, 0)
    later_mask = band_mask(2 * BLK, BLK)

    for g, (_, d) in enumerate(DILATED_CONFIG):
        for t in range(S // rt):
            rows = slice(t * rt, (t + 1) * rt)
            qs_ref[rows, :] = norm_rope(q_refs[g][rows, :], qn_ref[g], rows)
            ks_ref[rows, :] = norm_rope(k_refs[g][rows, :], kn_ref[g], rows)
        L = S // d
        sub = lambda rho, blk, size: pl.ds(rho + blk * BLK * d, size, stride=d)
        tiles = [(sub(rho, n, BLK),) + ((sub(rho, 0, BLK), first_mask) if n == 0 else
                                        (sub(rho, n - 1, 2 * BLK), later_mask))
                 for rho in range(d) for n in range(L // BLK)]
        for t0 in range(0, len(tiles), ATT_TILE_BATCH):
            batch = tiles[t0:t0 + ATT_TILE_BATCH]
            s = [jnp.where(mask, _dot_t(qs_ref[q_rows, :].astype(BF16),
                                        ks_ref[k_rows, :].astype(BF16)) * scale, -jnp.inf)
                 for q_rows, k_rows, mask in batch]
            m = [jnp.max(x, axis=-1, keepdims=True) for x in s]
            p = [jnp.exp(x - mx) for x, mx in zip(s, m)]
            l = [jnp.sum(x, axis=-1, keepdims=True) for x in p]
            o = [jnp.dot(x.astype(BF16), v_refs[g][k_rows, :].astype(BF16),
                         preferred_element_type=F32) for x, (_, k_rows, _) in zip(p, batch)]
            for i, (q_rows, _, _) in enumerate(batch):
                os_refs[g][q_rows, :] = o[i] / l[i]
                ls_refs[g][q_rows, :] = jnp.broadcast_to(m[i] + jnp.log(l[i]), (BLK, Dh))

    for t in range(S // rt):
        rows = slice(t * rt, (t + 1) * rt)
        lse = [ls_refs[g][rows, :] for g in range(G)]
        m = functools.reduce(jnp.maximum, lse)
        e = [jnp.exp(x - m) for x in lse]
        den = functools.reduce(lambda a, b: a + b, e)
        o = functools.reduce(lambda a, b: a + b,
                             [(e[g] / den) * os_refs[g][rows, :] for g in range(G)])
        gt = gate_ref[rows, :]
        out_ref[rows, :] = (o * (gt * jax.nn.sigmoid(gt))).astype(out_ref.dtype)


def _attn_mix(proj, tables, qn_w, kn_w, B, S):
    G = len(DILATED_CONFIG)
    H = ATT_HEADS
    Dh = ATT_HEAD_DIM
    col = lambda c0: pl.BlockSpec((None, S, Dh), lambda b, h, c0=c0: (c0 + h, b, 0))
    tab = pl.BlockSpec((None, S, Dh), lambda b, h: (b, 0, 0))
    nrm = pl.BlockSpec((G, 1, Dh), lambda b, h: (0, 0, 0))
    return pl.pallas_call(
        _attn_body,
        grid=(B, H),
        in_specs=[tab, tab, tab, nrm, nrm] + [col(c * H) for c in range(3 * G + 1)],
        out_specs=pl.BlockSpec((S, Dh), lambda b, h: (b, h)),
        out_shape=jax.ShapeDtypeStruct((B * S, H * Dh), BF16),
        scratch_shapes=[pltpu.VMEM((S, Dh), F32)] * (2 + 2 * G),
        compiler_params=_params(("parallel", "arbitrary")),
        name="dilated_attn",
    )(*tables, qn_w.reshape(G, 1, Dh), kn_w.reshape(G, 1, Dh), *([proj] * (3 * G + 1)))


def _attn_layer(x, positions, nw, w_in, qn_w, kn_w, w_out):
    B, S, D = x.shape
    M = B * S
    x2d = x.reshape(M, D)
    proj = _in_proj(x2d, nw, w_in.astype(BF16)[None], cols=True)
    og = _attn_mix(proj, _rope_tables(positions), qn_w, kn_w, B, S)
    return _out_proj(og, w_out.astype(BF16), x2d).reshape(B, S, D)


def _pool_body(u_ref, up_ref, g_ref, wg_ref, sc_ref, o_ref):
    s = pl.program_id(1)
    ts = u_ref.shape[0]
    Wg = wg_ref.shape[-1]
    t = s * ts + lax.broadcasted_iota(jnp.int32, (ts, 1), 0)
    for j, win in enumerate(POOL_WINDOWS):
        cols = slice(j * Wg, (j + 1) * Wg)
        u = u_ref[:, cols]
        halo = jnp.where(s == 0, 0.0, up_ref[:, cols])
        acc = jnp.concatenate([halo, u], axis=0)
        step = 1
        while step < win:
            acc = acc + pltpu.roll(acc, step, axis=0)
            step *= 2
        cnt = jnp.minimum(t + 1, win).astype(F32)
        diff = acc[POOL_HALO:, :] / cnt - u
        y = jnp.dot(diff.astype(BF16), wg_ref[j], preferred_element_type=F32) * sc_ref[:, cols]
        g = g_ref[:, cols]
        o_ref[:, cols] = (y * (g * jax.nn.sigmoid(g))).astype(o_ref.dtype)


def _pool_mix(proj, w_grp, scale, B, S, ts=256):
    W = proj.shape[-1] // 2
    ts = _tile(S, ts)
    hb = ts // POOL_HALO
    return pl.pallas_call(
        _pool_body,
        grid=(B, S // ts),
        in_specs=[pl.BlockSpec((None, ts, W), lambda b, s: (b, s, 0)),
                  pl.BlockSpec((None, POOL_HALO, W),
                               lambda b, s: (b, jnp.maximum(s * hb - 1, 0), 0)),
                  pl.BlockSpec((None, ts, W), lambda b, s: (b, s, 1)),
                  pl.BlockSpec(w_grp.shape, lambda b, s: (0, 0, 0)),
                  pl.BlockSpec((1, W), lambda b, s: (0, 0))],
        out_specs=pl.BlockSpec((None, ts, W), lambda b, s: (b, s, 0)),
        out_shape=jax.ShapeDtypeStruct((B, S, W), BF16),
        compiler_params=_params(("parallel", "arbitrary")),
        name="pool_mix",
    )(proj, proj, proj, w_grp.astype(BF16), scale.reshape(1, W))


def _pool_layer(x, nw, w_in, w_grp, scale, w_out):
    B, S, D = x.shape
    M = B * S
    x2d = x.reshape(M, D)
    proj = _in_proj(x2d, nw, w_in.astype(BF16)[None])[0]
    yg = _pool_mix(proj.reshape(B, S, -1), w_grp, scale, B, S)
    return _out_proj(yg.reshape(M, -1), w_out.astype(BF16), x2d).reshape(B, S, D)


def kernel(x, positions, norm_w, a_mu, a_w_in, a_w0, a_w1, a_w2, a_a0, a_a1, a_a2, a_k_k, a_k_a,
           a_r_k, a_lnx_w, a_lnx_b, a_v0, a_v1, a_v2, a_w_out, b_w_in, b_qn_w, b_kn_w, b_w_out,
           c_w_in, c_w_grp, c_scale, c_w_out):
    depth = norm_w.shape[0]
    ia = ib = ic = 0
    v_first = None
    for i in range(depth):
        kind = i % 3
        if kind == 0:
            vres = None if ia == 0 else (a_v0[ia - 1], a_v1[ia - 1], a_v2[ia - 1])
            x, v_first = _rwkv_layer(
                x, norm_w[i], a_mu[ia], a_w_in[ia], a_w0[ia], a_w1[ia], a_w2[ia], a_a0[ia],
                a_a1[ia], a_a2[ia], a_k_k[ia], a_k_a[ia], a_r_k[ia].reshape(-1), a_lnx_w[ia],
                a_lnx_b[ia], a_w_out[ia], v_first, vres)
            ia += 1
        elif kind == 1:
            x = _attn_layer(x, positions, norm_w[i], b_w_in[ib], b_qn_w[ib], b_kn_w[ib],
                            b_w_out[ib])
            ib += 1
        else:
            x = _pool_layer(x, norm_w[i], c_w_in[ic], c_w_grp[ic], c_scale[ic], c_w_out[ic])
            ic += 1
    return x
```

```python
import functools

import jax
import jax.numpy as jnp
from jax import lax
from jax.experimental import pallas as pl
from jax.experimental.pallas import tpu as pltpu

F32 = jnp.float32
BF16 = jnp.bfloat16

RMS_EPS = 1e-6
RWKV_HEAD_DIM = 64
RWKV_LN_EPS = 64e-5
DECAY_SCALE = 0.6065306597126334
ATT_HEAD_DIM = 128
ATT_HEADS = 8
ATT_BRANCH = ATT_HEADS * ATT_HEAD_DIM
DILATED_CONFIG = ((128, 1), (512, 4), (2048, 16))
ATT_BLOCK = 128
ATT_ROWS = 64
ATT_TILE_BATCH = 8
ROPE_THETA = 500000.0
ROPE_DIM = ATT_HEAD_DIM // 4
POOL_WINDOWS = (2, 4, 8, 16)
POOL_HALO = 16

V7X_LANES = 128
V7X_MXU_DIM = 256
V7X_VMEM_BYTES = 64 * 1024 * 1024
VMEM_LIMIT = 48 * 1024 * 1024

PREP_ROWS = 16
PROLOGUE_ROWS = 256
SCAN_CHUNK = 64
SCAN_LANES = V7X_MXU_DIM
SCAN_HEADS = SCAN_LANES // RWKV_HEAD_DIM
SCAN_DOUBLINGS = 5


def _params(sem):
    return pltpu.CompilerParams(dimension_semantics=sem, vmem_limit_bytes=VMEM_LIMIT)


def _tile(n, pref):
    t = min(n, pref)
    while n % t:
        t //= 2
    return t


def _rms(x, w):
    return (x * lax.rsqrt(jnp.mean(x * x, axis=-1, keepdims=True) + RMS_EPS)) * w


def _in_proj_body(*refs, cols):
    if len(refs) == 3:
        lhs_ref, w_ref, o_ref = refs
    else:
        x_ref, nw_ref, w_ref, o_ref, lhs_ref = refs
        tm = x_ref.shape[0]

        @pl.when(pl.program_id(2) == 0)
        def _():
            nw = nw_ref[...]
            for r0 in range(0, tm, PROLOGUE_ROWS):
                rows = slice(r0, r0 + PROLOGUE_ROWS)
                lhs_ref[rows, :] = _rms(x_ref[rows, :], nw).astype(BF16)

    acc = jnp.dot(lhs_ref[...], w_ref[...], preferred_element_type=F32)
    if cols:
        for c in range(o_ref.shape[0]):
            o_ref[c] = acc[:, c * V7X_LANES:(c + 1) * V7X_LANES]
    else:
        o_ref[...] = acc


def _in_proj(x, nw, w, cols=False, tm=1024, tn=1024):
    M, K = x.shape[-2:]
    G, _, N = w.shape
    tm = _tile(M, tm)
    tn = _tile(N, tn)
    w_spec = pl.BlockSpec((None, K, tn), lambda g, i, j: (g, 0, j))
    if nw is None:
        in_specs = [pl.BlockSpec((None, tm, K), lambda g, i, j: (g, i, 0)), w_spec]
        args = [x, w]
        scratch = []
    else:
        in_specs = [pl.BlockSpec((tm, K), lambda g, i, j: (i, 0)),
                    pl.BlockSpec((1, K), lambda g, i, j: (0, 0)), w_spec]
        args = [x, nw.reshape(1, K), w]
        scratch = [pltpu.VMEM((tm, K), BF16)]
    if cols:
        nc = tn // V7X_LANES
        out_spec = pl.BlockSpec((nc, tm, V7X_LANES), lambda g, i, j: (j, i, 0))
        out_shape = jax.ShapeDtypeStruct((N // V7X_LANES, M, V7X_LANES), F32)
    else:
        out_spec = pl.BlockSpec((None, tm, tn), lambda g, i, j: (g, i, j))
        out_shape = jax.ShapeDtypeStruct((G, M, N), F32)
    return pl.pallas_call(
        functools.partial(_in_proj_body, cols=cols),
        grid=(G, M // tm, N // tn),
        in_specs=in_specs,
        out_specs=out_spec,
        out_shape=out_shape,
        scratch_shapes=scratch,
        compiler_params=_params(("parallel", "parallel", "arbitrary")),
        name="in_proj",
    )(*args)


def _out_proj_body(a_ref, w_ref, r_ref, o_ref):
    o_ref[...] = r_ref[...] + jnp.dot(a_ref[...], w_ref[...], preferred_element_type=F32)


def _out_proj(a, w, res, tm=1024, tn=1024):
    M, K = a.shape
    N = w.shape[-1]
    tm = _tile(M, tm)
    tn = _tile(N, tn)
    tile = pl.BlockSpec((tm, tn), lambda i, j: (i, j))
    return pl.pallas_call(
        _out_proj_body,
        grid=(M // tm, N // tn),
        in_specs=[pl.BlockSpec((tm, K), lambda i, j: (i, 0)),
                  pl.BlockSpec((K, tn), lambda i, j: (0, j)), tile],
        out_specs=tile,
        out_shape=jax.ShapeDtypeStruct((M, N), F32),
        compiler_params=_params(("parallel", "arbitrary")),
        name="out_proj",
    )(a, w, res)


def _rwkv_prep_body(x_ref, xp_ref, nw_ref, mu_ref, o_ref):
    ts, D = x_ref.shape
    nb = D // V7X_LANES
    blk = lambda b: slice(b * V7X_LANES, (b + 1) * V7X_LANES)
    row = lax.broadcasted_iota(jnp.int32, (PREP_ROWS, V7X_LANES), 0)

    def inv_rms(blocks):
        ss = functools.reduce(lambda a, b: a + b, [x * x for x in blocks])
        return lax.rsqrt(jnp.sum(ss, axis=-1, keepdims=True) * (1.0 / D) + RMS_EPS)

    halo = [xp_ref[:, blk(b)] for b in range(nb)]
    s_halo = inv_rms(halo)
    carry = []
    for b in range(nb):
        hp = jnp.where(pl.program_id(1) == 0, 0.0, (halo[b] * s_halo) * nw_ref[:, blk(b)])
        carry.append(jnp.concatenate([pltpu.roll(hp, 1, axis=0), jnp.zeros_like(hp)], axis=0))

    def group(i, carry):
        rows = pl.ds(pl.multiple_of(i * PREP_ROWS, PREP_ROWS), PREP_ROWS)
        x = [x_ref[rows, blk(b)] for b in range(nb)]
        s = inv_rms(x)
        rolled = []
        for b in range(nb):
            h = (x[b] * s) * nw_ref[:, blk(b)]
            rolled.append(pltpu.roll(h, 1, axis=0))
            xx = jnp.where(row == 0, carry[b], rolled[b]) - h
            for j in range(6):
                o_ref[j, rows, blk(b)] = (h + xx * mu_ref[j:j + 1, blk(b)]).astype(o_ref.dtype)
        return tuple(rolled)

    lax.fori_loop(0, ts // PREP_ROWS, group, tuple(carry), unroll=4)


def _rwkv_prep(x, nw, mu, ts=256):
    B, S, D = x.shape
    ts = _tile(S, ts)
    sub = 8
    return pl.pallas_call(
        _rwkv_prep_body,
        grid=(B, S // ts),
        in_specs=[pl.BlockSpec((None, ts, D), lambda b, s: (b, s, 0)),
                  pl.BlockSpec((None, sub, D),
                               lambda b, s: (b, jnp.maximum(s * (ts // sub) - 1, 0), 0)),
                  pl.BlockSpec((1, D), lambda b, s: (0, 0)),
                  pl.BlockSpec((6, D), lambda b, s: (0, 0))],
        out_specs=pl.BlockSpec((6, None, ts, D), lambda b, s: (0, b, s, 0)),
        out_shape=jax.ShapeDtypeStruct((6, B, S, D), BF16),
        compiler_params=_params(("parallel", "arbitrary")),
        name="rwkv_prep",
    )(x, x, nw.reshape(1, D), mu)


def _rwkv_lora_body(*refs, has_v):
    if has_v:
        (xw_ref, xa_ref, xv_ref, v_ref, vf_ref,
         w1_ref, w2_ref, w0_ref, a1_ref, a2_ref, a0_ref, v1_ref, v2_ref, v0_ref,
         olw_ref, oa_ref, ov_ref) = refs
    else:
        (xw_ref, xa_ref, w1_ref, w2_ref, w0_ref, a1_ref, a2_ref, a0_ref,
         olw_ref, oa_ref) = refs
    t = jnp.tanh(jnp.dot(xw_ref[...], w1_ref[...], preferred_element_type=F32))
    z = w0_ref[...] + jnp.dot(t.astype(BF16), w2_ref[...], preferred_element_type=F32)
    olw_ref[...] = -DECAY_SCALE * jax.nn.sigmoid(z)
    ta = jnp.dot(xa_ref[...], a1_ref[...], preferred_element_type=F32)
    oa_ref[...] = jax.nn.sigmoid(
        a0_ref[...] + jnp.dot(ta.astype(BF16), a2_ref[...], preferred_element_type=F32))
    if has_v:
        tv = jnp.dot(xv_ref[...], v1_ref[...], preferred_element_type=F32)
        sv = jax.nn.sigmoid(
            v0_ref[...] + jnp.dot(tv.astype(BF16), v2_ref[...], preferred_element_type=F32))
        v = v_ref[...]
        ov_ref[...] = v + (vf_ref[...] - v) * sv


def _pad_lora(w1, w2):
    r = w1.shape[1]
    rp = -(-r // V7X_LANES) * V7X_LANES
    return (jnp.pad(w1, ((0, 0), (0, rp - r))).astype(BF16),
            jnp.pad(w2, ((0, rp - r), (0, 0))).astype(BF16))


def _rwkv_lora(mix, proj, v_first, w0, w1, w2, a0, a1, a2, vres, tm=256):
    _, M, D = mix.shape
    tm = _tile(M, tm)
    has_v = vres is not None
    row = lambda j: pl.BlockSpec((None, tm, D), lambda i, j=j: (j, i, 0))
    full = lambda a: pl.BlockSpec(a.shape, lambda i: (0,) * a.ndim)
    tile = pl.BlockSpec((tm, D), lambda i: (i, 0))
    w1p, w2p = _pad_lora(w1, w2)
    a1p, a2p = _pad_lora(a1, a2)
    weights = [w1p, w2p, w0.reshape(1, D), a1p, a2p, a0.reshape(1, D)]
    args = [mix, mix]
    in_specs = [row(4), row(5)]
    n_out = 2
    if has_v:
        v0, v1, v2 = vres
        v1p, v2p = _pad_lora(v1, v2)
        weights += [v1p, v2p, v0.reshape(1, D)]
        args += [mix, proj, v_first]
        in_specs += [row(2), row(2), row(2)]
        n_out = 3
    args += weights
    in_specs += [full(w) for w in weights]
    return pl.pallas_call(
        functools.partial(_rwkv_lora_body, has_v=has_v),
        grid=(M // tm,),
        in_specs=in_specs,
        out_specs=[tile] * n_out,
        out_shape=[jax.ShapeDtypeStruct((M, D), F32)] * n_out,
        compiler_params=_params(("parallel",)),
        name="rwkv_lora",
    )(*args)


def _split2(x):
    hi = x.astype(BF16)
    return hi, (x - hi.astype(F32)).astype(BF16)


def _dot_t(a, b):
    return lax.dot_general(a, b, (((1,), (1,)), ((), ())), preferred_element_type=F32)


def _dot_tl(a, b):
    return lax.dot_general(a, b, (((0,), (0,)), ((), ())), preferred_element_type=F32)


def _scan_body(r_ref, k_ref, v_ref, lw_ref, a_ref, g_ref,
               kk_ref, ka_ref, rk_ref, lnw_ref, lnb_ref, o_ref, state_ref, *, n_chunks):
    C, L, H, N = SCAN_CHUNK, SCAN_LANES, SCAN_HEADS, RWKV_HEAD_DIM
    n_rows, _, D = r_ref.shape
    ids = [(b, g) for b in range(n_rows) for g in range(D // L)]
    streams = range(len(ids))

    @pl.when(pl.program_id(1) == 0)
    def _():
        state_ref[...] = jnp.zeros_like(state_ref)

    lane_head = lax.broadcasted_iota(jnp.int32, (C, L), 1) // N
    head_masks = [lane_head == h for h in range(H)]
    t_idx = lax.broadcasted_iota(jnp.int32, (C, H * C), 0)
    s_idx = lax.broadcasted_iota(jnp.int32, (C, H * C), 1) % C
    strict_lower = s_idx < t_idx
    lower = s_idx <= t_idx
    eye = (s_idx == t_idx).astype(F32)
    bd_mask = (lax.broadcasted_iota(jnp.int32, (L, L), 0) // N
               == lax.broadcasted_iota(jnp.int32, (L, L), 1) // N)
    ones_bd = bd_mask.astype(BF16)
    eye_l = (lax.broadcasted_iota(jnp.int32, (L, L), 0)
             == lax.broadcasted_iota(jnp.int32, (L, L), 1))
    tri = (lax.broadcasted_iota(jnp.int32, (C, C), 1)
           <= lax.broadcasted_iota(jnp.int32, (C, C), 0)).astype(BF16)
    hc = H * C
    inv_n = 1.0 / N

    def blockstack(x):
        return jnp.concatenate([jnp.where(m, x, 0.0) for m in head_masks], axis=0).astype(BF16)

    def segsum(xs):
        s = jnp.dot(jnp.concatenate(xs, axis=0).astype(BF16), ones_bd, preferred_element_type=F32)
        return [s[g * C:(g + 1) * C] for g in range(len(xs))]

    def mm(a, b):
        return jnp.dot(a.astype(BF16), b, preferred_element_type=F32)

    lanes = lambda g: slice(g * L, (g + 1) * L)
    split = lambda ref: [ref[:, lanes(g)] for _, g in ids]
    k_k, k_a, r_k, ln_w, ln_b = (split(ref) for ref in (kk_ref, ka_ref, rk_ref, lnw_ref, lnb_ref))

    def chunk(ci, carry):
        sl = pl.ds(pl.multiple_of(ci * C, C), C)
        load = lambda ref: [ref[b, sl, lanes(g)] for b, g in ids]
        r, k, v, lw, a, gate = (load(ref) for ref in (r_ref, k_ref, v_ref, lw_ref, a_ref, g_ref))

        kk = [k[g] * k_k[g] for g in streams]
        n2 = segsum([x * x for x in kk])
        kk = [kk[g] / jnp.maximum(jnp.sqrt(n2[g]), 1e-12) for g in streams]
        k2 = [k[g] * (1.0 + (a[g] - 1.0) * k_a[g]) for g in streams]
        b_s = [kk[g] * a[g] for g in streams]

        cs = [jnp.dot(tri, jnp.concatenate(_split2(lw[g]), axis=1), preferred_element_type=F32)
              for g in streams]
        cum = [c[:, :L] + c[:, L:] for c in cs]
        total = [c[C - 1:C, :] for c in cum]
        at = [-kk[g] * jnp.exp(cum[g] - lw[g]) for g in streams]
        rt = [r[g] * jnp.exp(cum[g]) for g in streams]
        w_inv = [jnp.exp(-c) for c in cum]
        w_tail = [jnp.exp(total[g] - cum[g]) for g in streams]

        aa = [_dot_t(jnp.concatenate([at[g], rt[g]], axis=0).astype(BF16),
                     jnp.concatenate([blockstack(b_s[g] * w_inv[g]),
                                      blockstack(k2[g] * w_inv[g])], axis=0))
              for g in streams]
        a_ab = [jnp.where(strict_lower, x[:C, :hc], 0.0) for x in aa]
        a_ak = [jnp.where(strict_lower, x[:C, hc:], 0.0) for x in aa]
        a_rr = [jnp.concatenate([jnp.where(lower, x[C:, :hc], 0.0),
                                 jnp.where(lower, x[C:, hc:], 0.0)], axis=1) for x in aa]

        t_inv = [eye + x for x in a_ab]
        p = [mm(x, blockstack(x)) for x in a_ab]
        for i in range(SCAN_DOUBLINGS):
            bd = [blockstack(x) for x in p]
            if i + 1 < SCAN_DOUBLINGS:
                pt = [mm(jnp.concatenate([p[g], t_inv[g]], axis=0), bd[g]) for g in streams]
                p = [x[:C] for x in pt]
                t_inv = [t_inv[g] + pt[g][C:] for g in streams]
            else:
                t_inv = [t_inv[g] + mm(t_inv[g], bd[g]) for g in streams]

        state = [state_ref[g] for g in streams]
        bs_v = [blockstack(x) for x in v]
        zy = [mm(jnp.concatenate([at[g], rt[g]], axis=0), state[g].astype(BF16))
              for g in streams]
        z = [zy[g][:C] + mm(a_ak[g], bs_v[g]) for g in streams]
        u = [mm(t_inv[g], blockstack(z[g])) for g in streams]
        y = [zy[g][C:] + mm(a_rr[g], jnp.concatenate([blockstack(u[g]), bs_v[g]], axis=0))
             for g in streams]
        upd = [_dot_tl(jnp.concatenate([b_s[g] * w_tail[g], k2[g] * w_tail[g]],
                                       axis=0).astype(BF16),
                       jnp.concatenate([u[g], v[g]], axis=0).astype(BF16)) for g in streams]
        for g in streams:
            decay = jnp.sum(jnp.where(eye_l, jnp.exp(total[g]), 0.0), axis=1, keepdims=True)
            state_ref[g] = state[g] * decay + jnp.where(bd_mask, upd[g], 0.0)

        mean = segsum(y)
        yc = [y[g] - mean[g] * inv_n for g in streams]
        var = segsum([x * x for x in yc])
        bonus = segsum([r[g] * k2[g] * r_k[g] for g in streams])
        for g in streams:
            yn = (yc[g] * lax.rsqrt(var[g] * inv_n + RWKV_LN_EPS)) * ln_w[g] + ln_b[g]
            out = (yn + bonus[g] * v[g]) * (gate[g] * jax.nn.sigmoid(gate[g]))
            o_ref[ids[g][0], sl, lanes(ids[g][1])] = out.astype(o_ref.dtype)
        return carry

    lax.fori_loop(0, n_chunks, chunk, 0)


def _rwkv_scan(proj, v, v_idx, lw, a, k_k, k_a, r_k, ln_w, ln_b, B, S, ts=128, n_rows=2):
    _, M, D = proj.shape
    ts = _tile(S, ts)
    n_rows = _tile(B, n_rows)
    L = SCAN_LANES
    pj = lambda j: pl.BlockSpec((None, n_rows, ts, D), lambda b, c, j=j: (j, b, c, 0))
    tile = pl.BlockSpec((n_rows, ts, D), lambda b, c: (b, c, 0))
    par = pl.BlockSpec((1, D), lambda b, c: (0, 0))
    vec = lambda p: p.reshape(1, D)
    seq = lambda t: t.reshape(t.shape[:-2] + (B, S, D))
    return pl.pallas_call(
        functools.partial(_scan_body, n_chunks=ts // SCAN_CHUNK),
        grid=(B // n_rows, S // ts),
        in_specs=[pj(0), pj(1), pj(v_idx), tile, tile, pj(3), par, par, par, par, par],
        out_specs=tile,
        out_shape=jax.ShapeDtypeStruct((B, S, D), BF16),
        scratch_shapes=[pltpu.VMEM((n_rows * (D // L), L, L), F32)],
        compiler_params=_params(("parallel", "arbitrary")),
        name="rwkv_scan",
    )(seq(proj), seq(proj), seq(v), seq(lw), seq(a), seq(proj),
      vec(k_k), vec(k_a), vec(r_k), vec(ln_w), vec(ln_b)).reshape(M, D)


def _rwkv_layer(x, nw, mu, w_in, w0, w1, w2, a0, a1, a2, k_k, k_a, r_k, ln_w, ln_b, w_out,
                v_first, vres):
    B, S, D = x.shape
    M = B * S
    mix = _rwkv_prep(x, nw, mu).reshape(6, M, D)
    proj = _in_proj(mix, None, w_in.astype(BF16))
    outs = _rwkv_lora(mix, proj, v_first, w0, w1, w2, a0, a1, a2, vres)
    if vres is None:
        lw, a = outs
        v, v_idx = proj, 2
        v_first = proj
    else:
        lw, a, v = outs
        v, v_idx = v[None], 0
    yg = _rwkv_scan(proj, v, v_idx, lw, a, k_k, k_a, r_k, ln_w, ln_b, B, S)
    return _out_proj(yg, w_out.astype(BF16), x.reshape(M, D)).reshape(B, S, D), v_first


def _rope_body(pos_ref, invf_ref, cos_ref, sin_ref):
    ang = pos_ref[...].astype(F32) * invf_ref[...]
    cos_ref[...] = jnp.cos(ang)
    sin_ref[...] = jnp.sin(ang)


def _rope_tables(positions, ts=512):
    B, S = positions.shape
    Dh = ATT_HEAD_DIM
    ts = _tile(S, ts)
    inv_freq = ROPE_THETA ** (-jnp.arange(0, ROPE_DIM, 2, dtype=F32) / ROPE_DIM)
    invf = jnp.zeros((Dh,), F32).at[:ROPE_DIM].set(jnp.tile(inv_freq, 2)).reshape(1, Dh)
    out = pl.BlockSpec((None, ts, Dh), lambda b, s: (b, s, 0))
    shp = jax.ShapeDtypeStruct((B, S, Dh), F32)
    return pl.pallas_call(
        _rope_body,
        grid=(B, S // ts),
        in_specs=[pl.BlockSpec((None, ts, 1), lambda b, s: (b, s, 0)),
                  pl.BlockSpec((1, Dh), lambda b, s: (0, 0))],
        out_specs=[out] * 2,
        out_shape=[shp] * 2,
        compiler_params=_params(("parallel", "arbitrary")),
        name="rope_tables",
    )(positions.reshape(B, S, 1), invf)


def _attn_body(cos_ref, sin_ref, qn_ref, kn_ref, *refs):
    G = len(DILATED_CONFIG)
    q_refs, k_refs, v_refs = refs[:G], refs[G:2 * G], refs[2 * G:3 * G]
    gate_ref, out_ref, qs_ref, ks_ref = refs[3 * G:3 * G + 4]
    os_refs, ls_refs = refs[3 * G + 4:4 * G + 4], refs[4 * G + 4:]
    S, Dh = qs_ref.shape
    BLK = ATT_BLOCK
    half = ROPE_DIM // 2
    scale = Dh ** -0.5

    ri = lax.broadcasted_iota(jnp.int32, (Dh, Dh), 0)
    rj = lax.broadcasted_iota(jnp.int32, (Dh, Dh), 1)
    rot_half = (jnp.where((rj >= half) & (rj < ROPE_DIM) & (ri == rj - half), 1.0, 0.0)
                - jnp.where((rj < half) & (ri == rj + half), 1.0, 0.0)).astype(BF16)
    ones_k = jnp.ones((2 * BLK, Dh), BF16)

    def norm_rope(x, w):
        xn = _rms(x, w)
        parts = jnp.dot(jnp.concatenate(_split2(xn), axis=0), rot_half,
                        preferred_element_type=F32)
        return xn * cos_ref[...] + (parts[:S] + parts[S:]) * sin_ref[...]

    def band_mask(kw, off):
        qi = lax.broadcasted_iota(jnp.int32, (BLK, kw), 0)
        kj = lax.broadcasted_iota(jnp.int32, (BLK, kw), 1)
        return (kj >= qi + (off - BLK)) & (kj <= qi + off)

    first_mask = band_mask(BLK, 0)
    later_mask = band_mask(2 * BLK, BLK)

    for g, (_, d) in enumerate(DILATED_CONFIG):
        qs_ref[...] = norm_rope(q_refs[g][...], qn_ref[g]) * scale
        ks_ref[...] = norm_rope(k_refs[g][...], kn_ref[g])
        L = S // d
        sub = lambda rho, blk, size: pl.ds(rho + blk * BLK * d, size, stride=d)
        tiles = [(sub(rho, n, BLK),) + ((sub(rho, 0, BLK), first_mask) if n == 0 else
                                        (sub(rho, n - 1, 2 * BLK), later_mask))
                 for rho in range(d) for n in range(L // BLK)]
        for t0 in range(0, len(tiles), ATT_TILE_BATCH):
            batch = tiles[t0:t0 + ATT_TILE_BATCH]
            s = [jnp.where(mask, _dot_t(qs_ref[q_rows, :].astype(BF16),
                                        ks_ref[k_rows, :].astype(BF16)), -jnp.inf)
                 for q_rows, k_rows, mask in batch]
            m = [jnp.max(x, axis=-1, keepdims=True) for x in s]
            p = [jnp.exp(x - mx).astype(BF16) for x, mx in zip(s, m)]
            l = [jnp.dot(x, ones_k[:x.shape[1]], preferred_element_type=F32) for x in p]
            o = [jnp.dot(x, v_refs[g][k_rows, :].astype(BF16), preferred_element_type=F32)
                 for x, (_, k_rows, _) in zip(p, batch)]
            for i, (q_rows, _, _) in enumerate(batch):
                os_refs[g][q_rows, :] = o[i] / l[i]
                ls_refs[g][q_rows, :] = m[i] + jnp.log(l[i])

    def mix_rows(i, carry):
        rows = pl.ds(pl.multiple_of(i * ATT_ROWS, ATT_ROWS), ATT_ROWS)
        lse = [ls_refs[g][rows, :] for g in range(G)]
        m = functools.reduce(jnp.maximum, lse)
        e = [jnp.exp(x - m) for x in lse]
        den = functools.reduce(lambda a, b: a + b, e)
        o = functools.reduce(lambda a, b: a + b,
                             [(e[g] / den) * os_refs[g][rows, :] for g in range(G)])
        gt = gate_ref[rows, :]
        out_ref[rows, :] = (o * (gt * jax.nn.sigmoid(gt))).astype(out_ref.dtype)
        return carry

    lax.fori_loop(0, S // ATT_ROWS, mix_rows, 0, unroll=2)


def _attn_mix(proj, tables, qn_w, kn_w, B, S):
    G = len(DILATED_CONFIG)
    H = ATT_HEADS
    Dh = ATT_HEAD_DIM
    col = lambda c0: pl.BlockSpec((None, S, Dh), lambda b, h, c0=c0: (c0 + h, b, 0))
    tab = pl.BlockSpec((None, S, Dh), lambda b, h: (b, 0, 0))
    nrm = pl.BlockSpec((G, 1, Dh), lambda b, h: (0, 0, 0))
    return pl.pallas_call(
        _attn_body,
        grid=(B, H),
        in_specs=[tab, tab, nrm, nrm] + [col(c * H) for c in range(3 * G + 1)],
        out_specs=pl.BlockSpec((S, Dh), lambda b, h: (b, h)),
        out_shape=jax.ShapeDtypeStruct((B * S, H * Dh), BF16),
        scratch_shapes=[pltpu.VMEM((S, Dh), F32)] * (2 + 2 * G),
        compiler_params=_params(("parallel", "arbitrary")),
        name="dilated_attn",
    )(*tables, qn_w.reshape(G, 1, Dh), kn_w.reshape(G, 1, Dh), *([proj] * (3 * G + 1)))


def _attn_layer(x, positions, nw, w_in, qn_w, kn_w, w_out):
    B, S, D = x.shape
    M = B * S
    x2d = x.reshape(M, D)
    proj = _in_proj(x2d, nw, w_in.astype(BF16)[None], cols=True)
    og = _attn_mix(proj, _rope_tables(positions), qn_w, kn_w, B, S)
    return _out_proj(og, w_out.astype(BF16), x2d).reshape(B, S, D)


def _pool_body(u_ref, up_ref, g_ref, wg_ref, sc_ref, o_ref):
    s = pl.program_id(1)
    ts = u_ref.shape[0]
    Wg = wg_ref.shape[-1]
    t = s * ts + lax.broadcasted_iota(jnp.int32, (ts, 1), 0)
    for j, win in enumerate(POOL_WINDOWS):
        cols = slice(j * Wg, (j + 1) * Wg)
        u = u_ref[:, cols]
        halo = jnp.where(s == 0, 0.0, up_ref[:, cols])
        acc = jnp.concatenate([halo, u], axis=0)
        step = 1
        while step < win:
            acc = acc + pltpu.roll(acc, step, axis=0)
            step *= 2
        cnt = jnp.minimum(t + 1, win).astype(F32)
        diff = acc[POOL_HALO:, :] / cnt - u
        y = jnp.dot(diff.astype(BF16), wg_ref[j], preferred_element_type=F32) * sc_ref[:, cols]
        g = g_ref[:, cols]
        o_ref[:, cols] = (y * (g * jax.nn.sigmoid(g))).astype(o_ref.dtype)


def _pool_mix(proj, w_grp, scale, B, S, ts=256):
    W = proj.shape[-1] // 2
    ts = _tile(S, ts)
    hb = ts // POOL_HALO
    return pl.pallas_call(
        _pool_body,
        grid=(B, S // ts),
        in_specs=[pl.BlockSpec((None, ts, W), lambda b, s: (b, s, 0)),
                  pl.BlockSpec((None, POOL_HALO, W),
                               lambda b, s: (b, jnp.maximum(s * hb - 1, 0), 0)),
                  pl.BlockSpec((None, ts, W), lambda b, s: (b, s, 1)),
                  pl.BlockSpec(w_grp.shape, lambda b, s: (0, 0, 0)),
                  pl.BlockSpec((1, W), lambda b, s: (0, 0))],
        out_specs=pl.BlockSpec((None, ts, W), lambda b, s: (b, s, 0)),
        out_shape=jax.ShapeDtypeStruct((B, S, W), BF16),
        compiler_params=_params(("parallel", "arbitrary")),
        name="pool_mix",
    )(proj, proj, proj, w_grp.astype(BF16), scale.reshape(1, W))


def _pool_layer(x, nw, w_in, w_grp, scale, w_out):
    B, S, D = x.shape
    M = B * S
    x2d = x.reshape(M, D)
    proj = _in_proj(x2d, nw, w_in.astype(BF16)[None])[0]
    yg = _pool_mix(proj.reshape(B, S, -1), w_grp, scale, B, S)
    return _out_proj(yg.reshape(M, -1), w_out.astype(BF16), x2d).reshape(B, S, D)


def kernel(x, positions, norm_w, a_mu, a_w_in, a_w0, a_w1, a_w2, a_a0, a_a1, a_a2, a_k_k, a_k_a,
           a_r_k, a_lnx_w, a_lnx_b, a_v0, a_v1, a_v2, a_w_out, b_w_in, b_qn_w, b_kn_w, b_w_out,
           c_w_in, c_w_grp, c_scale, c_w_out):
    depth = norm_w.shape[0]
    ia = ib = ic = 0
    v_first = None
    for i in range(depth):
        kind = i % 3
        if kind == 0:
            vres = None if ia == 0 else (a_v0[ia - 1], a_v1[ia - 1], a_v2[ia - 1])
            x, v_first = _rwkv_layer(
                x, norm_w[i], a_mu[ia], a_w_in[ia], a_w0[ia], a_w1[ia], a_w2[ia], a_a0[ia],
                a_a1[ia], a_a2[ia], a_k_k[ia], a_k_a[ia], a_r_k[ia].reshape(-1), a_lnx_w[ia],
                a_lnx_b[ia], a_w_out[ia], v_first, vres)
            ia += 1
        elif kind == 1:
            x = _attn_layer(x, positions, norm_w[i], b_w_in[ib], b_qn_w[ib], b_kn_w[ib],
                            b_w_out[ib])
            ib += 1
        else:
            x = _pool_layer(x, norm_w[i], c_w_in[ic], c_w_grp[ic], c_scale[ic], c_w_out[ic])
            ic += 1
    return x
```

```python
import functools

import jax
import jax.numpy as jnp
from jax import lax
from jax.experimental import pallas as pl
from jax.experimental.pallas import tpu as pltpu

F32 = jnp.float32
BF16 = jnp.bfloat16

RMS_EPS = 1e-6
RWKV_HEAD_DIM = 64
RWKV_LN_EPS = 64e-5
DECAY_SCALE = 0.6065306597126334
ATT_HEAD_DIM = 128
ATT_HEADS = 8
ATT_BRANCH = ATT_HEADS * ATT_HEAD_DIM
DILATED_CONFIG = ((128, 1), (512, 4), (2048, 16))
ATT_BLOCK = 128
ATT_ROWS = 64
ATT_TILE_BATCH = 8
ROPE_THETA = 500000.0
ROPE_DIM = ATT_HEAD_DIM // 4
POOL_WINDOWS = (2, 4, 8, 16)
POOL_HALO = 16

V7X_LANES = 128
V7X_MXU_DIM = 256
V7X_VMEM_BYTES = 64 * 1024 * 1024
VMEM_LIMIT = 56 * 1024 * 1024

PREP_ROWS = 16
PROLOGUE_ROWS = 256
SCAN_CHUNK = 64
SCAN_LANES = V7X_MXU_DIM
SCAN_HEADS = SCAN_LANES // RWKV_HEAD_DIM
SCAN_DOUBLINGS = 5


def _params(sem):
    return pltpu.CompilerParams(dimension_semantics=sem, vmem_limit_bytes=VMEM_LIMIT)


def _tile(n, pref):
    t = min(n, pref)
    while n % t:
        t //= 2
    return t


def _rms(x, w):
    return (x * lax.rsqrt(jnp.mean(x * x, axis=-1, keepdims=True) + RMS_EPS)) * w


def _in_proj_body(*refs, cols):
    if len(refs) == 3:
        lhs_ref, w_ref, o_ref = refs
    else:
        x_ref, nw_ref, w_ref, o_ref, lhs_ref = refs
        tm = x_ref.shape[0]

        @pl.when(pl.program_id(2) == 0)
        def _():
            nw = nw_ref[...]
            for r0 in range(0, tm, PROLOGUE_ROWS):
                rows = slice(r0, r0 + PROLOGUE_ROWS)
                lhs_ref[rows, :] = _rms(x_ref[rows, :], nw).astype(BF16)

    acc = jnp.dot(lhs_ref[...], w_ref[...], preferred_element_type=F32)
    if cols:
        for c in range(o_ref.shape[0]):
            o_ref[c] = acc[:, c * V7X_LANES:(c + 1) * V7X_LANES]
    else:
        o_ref[...] = acc


def _in_proj(x, nw, w, cols=False, tm=1024, tn=1024):
    M, K = x.shape[-2:]
    G, _, N = w.shape
    tm = _tile(M, tm)
    tn = _tile(N, tn)
    w_spec = pl.BlockSpec((None, K, tn), lambda g, i, j: (g, 0, j))
    if nw is None:
        in_specs = [pl.BlockSpec((None, tm, K), lambda g, i, j: (g, i, 0)), w_spec]
        args = [x, w]
        scratch = []
    else:
        in_specs = [pl.BlockSpec((tm, K), lambda g, i, j: (i, 0)),
                    pl.BlockSpec((1, K), lambda g, i, j: (0, 0)), w_spec]
        args = [x, nw.reshape(1, K), w]
        scratch = [pltpu.VMEM((tm, K), BF16)]
    if cols:
        nc = tn // V7X_LANES
        out_spec = pl.BlockSpec((nc, tm, V7X_LANES), lambda g, i, j: (j, i, 0))
        out_shape = jax.ShapeDtypeStruct((N // V7X_LANES, M, V7X_LANES), F32)
    else:
        out_spec = pl.BlockSpec((None, tm, tn), lambda g, i, j: (g, i, j))
        out_shape = jax.ShapeDtypeStruct((G, M, N), F32)
    return pl.pallas_call(
        functools.partial(_in_proj_body, cols=cols),
        grid=(G, M // tm, N // tn),
        in_specs=in_specs,
        out_specs=out_spec,
        out_shape=out_shape,
        scratch_shapes=scratch,
        compiler_params=_params(("parallel", "parallel", "arbitrary")),
        name="in_proj",
    )(*args)


def _out_proj_body(a_ref, w_ref, r_ref, o_ref):
    o_ref[...] = r_ref[...] + jnp.dot(a_ref[...], w_ref[...], preferred_element_type=F32)


def _out_proj(a, w, res, tm=512, tn=2048):
    M, K = a.shape
    N = w.shape[-1]
    tm = _tile(M, tm)
    tn = _tile(N, tn)
    tile = pl.BlockSpec((tm, tn), lambda i, j: (i, j))
    return pl.pallas_call(
        _out_proj_body,
        grid=(M // tm, N // tn),
        in_specs=[pl.BlockSpec((tm, K), lambda i, j: (i, 0)),
                  pl.BlockSpec((K, tn), lambda i, j: (0, j)), tile],
        out_specs=tile,
        out_shape=jax.ShapeDtypeStruct((M, N), F32),
        compiler_params=_params(("parallel", "arbitrary")),
        name="out_proj",
    )(a, w, res)


def _rwkv_prep_body(x_ref, xp_ref, nw_ref, mu_ref, o_ref):
    ts, D = x_ref.shape
    nb = D // V7X_LANES
    blk = lambda b: slice(b * V7X_LANES, (b + 1) * V7X_LANES)
    row = lax.broadcasted_iota(jnp.int32, (PREP_ROWS, V7X_LANES), 0)

    def inv_rms(blocks):
        ss = functools.reduce(lambda a, b: a + b, [x * x for x in blocks])
        return lax.rsqrt(jnp.sum(ss, axis=-1, keepdims=True) * (1.0 / D) + RMS_EPS)

    halo = [xp_ref[:, blk(b)] for b in range(nb)]
    s_halo = inv_rms(halo)
    carry = []
    for b in range(nb):
        hp = jnp.where(pl.program_id(1) == 0, 0.0, (halo[b] * s_halo) * nw_ref[:, blk(b)])
        carry.append(jnp.concatenate([pltpu.roll(hp, 1, axis=0), jnp.zeros_like(hp)], axis=0))

    def group(i, carry):
        rows = pl.ds(pl.multiple_of(i * PREP_ROWS, PREP_ROWS), PREP_ROWS)
        x = [x_ref[rows, blk(b)] for b in range(nb)]
        s = inv_rms(x)
        rolled = []
        for b in range(nb):
            h = (x[b] * s) * nw_ref[:, blk(b)]
            rolled.append(pltpu.roll(h, 1, axis=0))
            xx = jnp.where(row == 0, carry[b], rolled[b]) - h
            for j in range(6):
                o_ref[j, rows, blk(b)] = (h + xx * mu_ref[j:j + 1, blk(b)]).astype(o_ref.dtype)
        return tuple(rolled)

    lax.fori_loop(0, ts // PREP_ROWS, group, tuple(carry), unroll=4)


def _rwkv_prep(x, nw, mu, ts=256):
    B, S, D = x.shape
    ts = _tile(S, ts)
    sub = 8
    return pl.pallas_call(
        _rwkv_prep_body,
        grid=(B, S // ts),
        in_specs=[pl.BlockSpec((None, ts, D), lambda b, s: (b, s, 0)),
                  pl.BlockSpec((None, sub, D),
                               lambda b, s: (b, jnp.maximum(s * (ts // sub) - 1, 0), 0)),
                  pl.BlockSpec((1, D), lambda b, s: (0, 0)),
                  pl.BlockSpec((6, D), lambda b, s: (0, 0))],
        out_specs=pl.BlockSpec((6, None, ts, D), lambda b, s: (0, b, s, 0)),
        out_shape=jax.ShapeDtypeStruct((6, B, S, D), BF16),
        compiler_params=_params(("parallel", "arbitrary")),
        name="rwkv_prep",
    )(x, x, nw.reshape(1, D), mu)


def _rwkv_lora_body(*refs, n_branch):
    x_refs, w_refs, o_refs = refs[:n_branch], refs[n_branch:4 * n_branch], refs[4 * n_branch:]
    for b in range(n_branch):
        w1_ref, w2_ref, w0_ref = w_refs[3 * b:3 * b + 3]
        t = jnp.dot(x_refs[b][...], w1_ref[...], preferred_element_type=F32)
        if b == 0:
            t = jnp.tanh(t)
        gate = jax.nn.sigmoid(
            w0_ref[...] + jnp.dot(t.astype(BF16), w2_ref[...], preferred_element_type=F32))
        o_refs[b][...] = -DECAY_SCALE * gate if b == 0 else gate


def _pad_lora(w1, w2):
    r = w1.shape[1]
    rp = -(-r // V7X_LANES) * V7X_LANES
    return (jnp.pad(w1, ((0, 0), (0, rp - r))).astype(BF16),
            jnp.pad(w2, ((0, rp - r), (0, 0))).astype(BF16))


def _rwkv_lora(mix, w0, w1, w2, a0, a1, a2, vres, tm=256):
    _, M, D = mix.shape
    tm = _tile(M, tm)
    branches = [(4, w1, w2, w0), (5, a1, a2, a0)]
    if vres is not None:
        v0, v1, v2 = vres
        branches.append((2, v1, v2, v0))
    row = lambda j: pl.BlockSpec((None, tm, D), lambda i, j=j: (j, i, 0))
    full = lambda a: pl.BlockSpec(a.shape, lambda i: (0,) * a.ndim)
    tile = pl.BlockSpec((tm, D), lambda i: (i, 0))
    weights = [w for _, down, up, bias in branches
               for w in (*_pad_lora(down, up), bias.reshape(1, D))]
    return pl.pallas_call(
        functools.partial(_rwkv_lora_body, n_branch=len(branches)),
        grid=(M // tm,),
        in_specs=[row(j) for j, *_ in branches] + [full(w) for w in weights],
        out_specs=[tile] * len(branches),
        out_shape=[jax.ShapeDtypeStruct((M, D), F32)] * len(branches),
        compiler_params=_params(("parallel",)),
        name="rwkv_lora",
    )(*([mix] * len(branches)), *weights)


def _split2(x):
    hi = x.astype(BF16)
    return hi, (x - hi.astype(F32)).astype(BF16)


def _dot_t(a, b):
    return lax.dot_general(a, b, (((1,), (1,)), ((), ())), preferred_element_type=F32)


def _dot_tl(a, b):
    return lax.dot_general(a, b, (((0,), (0,)), ((), ())), preferred_element_type=F32)


def _scan_body(*refs, n_chunks, has_vres):
    r_ref, k_ref, v_ref, lw_ref, a_ref, g_ref = refs[:6]
    vf_ref, sv_ref = refs[6:8] if has_vres else (None, None)
    kk_ref, ka_ref, rk_ref, lnw_ref, lnb_ref, o_ref, state_ref = refs[-7:]
    C, L, H, N = SCAN_CHUNK, SCAN_LANES, SCAN_HEADS, RWKV_HEAD_DIM
    n_rows, _, D = r_ref.shape
    ids = [(b, g) for b in range(n_rows) for g in range(D // L)]
    streams = range(len(ids))

    @pl.when(pl.program_id(1) == 0)
    def _():
        state_ref[...] = jnp.zeros_like(state_ref)

    lane_head = lax.broadcasted_iota(jnp.int32, (C, L), 1) // N
    head_masks = [lane_head == h for h in range(H)]
    t_idx = lax.broadcasted_iota(jnp.int32, (C, H * C), 0)
    s_idx = lax.broadcasted_iota(jnp.int32, (C, H * C), 1) % C
    strict_lower = s_idx < t_idx
    lower = s_idx <= t_idx
    eye = (s_idx == t_idx).astype(F32)
    bd_mask = (lax.broadcasted_iota(jnp.int32, (L, L), 0) // N
               == lax.broadcasted_iota(jnp.int32, (L, L), 1) // N)
    ones_bd = bd_mask.astype(BF16)
    eye_l = (lax.broadcasted_iota(jnp.int32, (L, L), 0)
             == lax.broadcasted_iota(jnp.int32, (L, L), 1))
    tri = (lax.broadcasted_iota(jnp.int32, (C, C), 1)
           <= lax.broadcasted_iota(jnp.int32, (C, C), 0)).astype(BF16)
    hc = H * C
    inv_n = 1.0 / N

    def blockstack(x):
        return jnp.concatenate([jnp.where(m, x, 0.0) for m in head_masks], axis=0).astype(BF16)

    def segsum(xs):
        s = jnp.dot(jnp.concatenate(xs, axis=0).astype(BF16), ones_bd, preferred_element_type=F32)
        return [s[g * C:(g + 1) * C] for g in range(len(xs))]

    def mm(a, b):
        return jnp.dot(a.astype(BF16), b, preferred_element_type=F32)

    lanes = lambda g: slice(g * L, (g + 1) * L)
    split = lambda ref: [ref[:, lanes(g)] for _, g in ids]
    k_k, k_a, r_k, ln_w, ln_b = (split(ref) for ref in (kk_ref, ka_ref, rk_ref, lnw_ref, lnb_ref))

    def chunk(ci, carry):
        sl = pl.ds(pl.multiple_of(ci * C, C), C)
        load = lambda ref: [ref[b, sl, lanes(g)] for b, g in ids]
        r, k, v, lw, a, gate = (load(ref) for ref in (r_ref, k_ref, v_ref, lw_ref, a_ref, g_ref))
        if has_vres:
            v = [x + (vf - x) * sv for x, vf, sv in zip(v, load(vf_ref), load(sv_ref))]

        kk = [k[g] * k_k[g] for g in streams]
        n2 = segsum([x * x for x in kk])
        kk = [kk[g] / jnp.maximum(jnp.sqrt(n2[g]), 1e-12) for g in streams]
        k2 = [k[g] * (1.0 + (a[g] - 1.0) * k_a[g]) for g in streams]
        b_s = [kk[g] * a[g] for g in streams]

        cs = [jnp.dot(tri, jnp.concatenate(_split2(lw[g]), axis=1), preferred_element_type=F32)
              for g in streams]
        cum = [c[:, :L] + c[:, L:] for c in cs]
        total = [c[C - 1:C, :] for c in cum]
        at = [-kk[g] * jnp.exp(cum[g] - lw[g]) for g in streams]
        rt = [r[g] * jnp.exp(cum[g]) for g in streams]
        w_inv = [jnp.exp(-c) for c in cum]
        w_tail = [jnp.exp(total[g] - cum[g]) for g in streams]

        aa = [_dot_t(jnp.concatenate([at[g], rt[g]], axis=0).astype(BF16),
                     jnp.concatenate([blockstack(b_s[g] * w_inv[g]),
                                      blockstack(k2[g] * w_inv[g])], axis=0))
              for g in streams]
        a_ab = [jnp.where(strict_lower, x[:C, :hc], 0.0) for x in aa]
        a_ak = [jnp.where(strict_lower, x[:C, hc:], 0.0) for x in aa]
        a_rr = [jnp.concatenate([jnp.where(lower, x[C:, :hc], 0.0),
                                 jnp.where(lower, x[C:, hc:], 0.0)], axis=1) for x in aa]

        t_inv = [eye + x for x in a_ab]
        p = [mm(x, blockstack(x)) for x in a_ab]
        for i in range(SCAN_DOUBLINGS):
            bd = [blockstack(x) for x in p]
            if i + 1 < SCAN_DOUBLINGS:
                pt = [mm(jnp.concatenate([p[g], t_inv[g]], axis=0), bd[g]) for g in streams]
                p = [x[:C] for x in pt]
                t_inv = [t_inv[g] + pt[g][C:] for g in streams]
            else:
                t_inv = [t_inv[g] + mm(t_inv[g], bd[g]) for g in streams]

        state = [state_ref[g] for g in streams]
        bs_v = [blockstack(x) for x in v]
        zy = [mm(jnp.concatenate([at[g], rt[g]], axis=0), state[g].astype(BF16))
              for g in streams]
        z = [zy[g][:C] + mm(a_ak[g], bs_v[g]) for g in streams]
        u = [mm(t_inv[g], blockstack(z[g])) for g in streams]
        y = [zy[g][C:] + mm(a_rr[g], jnp.concatenate([blockstack(u[g]), bs_v[g]], axis=0))
             for g in streams]
        upd = [_dot_tl(jnp.concatenate([b_s[g] * w_tail[g], k2[g] * w_tail[g]],
                                       axis=0).astype(BF16),
                       jnp.concatenate([u[g], v[g]], axis=0).astype(BF16)) for g in streams]
        for g in streams:
            decay = jnp.sum(jnp.where(eye_l, jnp.exp(total[g]), 0.0), axis=1, keepdims=True)
            state_ref[g] = state[g] * decay + jnp.where(bd_mask, upd[g], 0.0)

        mean = segsum(y)
        yc = [y[g] - mean[g] * inv_n for g in streams]
        var = segsum([x * x for x in yc])
        bonus = segsum([r[g] * k2[g] * r_k[g] for g in streams])
        for g in streams:
            yn = (yc[g] * lax.rsqrt(var[g] * inv_n + RWKV_LN_EPS)) * ln_w[g] + ln_b[g]
            out = (yn + bonus[g] * v[g]) * (gate[g] * jax.nn.sigmoid(gate[g]))
            o_ref[ids[g][0], sl, lanes(ids[g][1])] = out.astype(o_ref.dtype)
        return carry

    lax.fori_loop(0, n_chunks, chunk, 0)


def _rwkv_scan(proj, lw, a, vres, k_k, k_a, r_k, ln_w, ln_b, B, S, ts=128, n_rows=2):
    _, M, D = proj.shape
    ts = _tile(S, ts)
    n_rows = _tile(B, n_rows)
    L = SCAN_LANES
    pj = lambda j: pl.BlockSpec((None, n_rows, ts, D), lambda b, c, j=j: (j, b, c, 0))
    tile = pl.BlockSpec((n_rows, ts, D), lambda b, c: (b, c, 0))
    par = pl.BlockSpec((1, D), lambda b, c: (0, 0))
    vec = lambda p: p.reshape(1, D)
    seq = lambda t: t.reshape(t.shape[:-2] + (B, S, D))
    args = [seq(proj), seq(proj), seq(proj), seq(lw), seq(a), seq(proj)]
    in_specs = [pj(0), pj(1), pj(2), tile, tile, pj(3)]
    if vres is not None:
        args += [seq(vres[0]), seq(vres[1])]
        in_specs += [pj(2), tile]
    args += [vec(k_k), vec(k_a), vec(r_k), vec(ln_w), vec(ln_b)]
    in_specs += [par] * 5
    return pl.pallas_call(
        functools.partial(_scan_body, n_chunks=ts // SCAN_CHUNK, has_vres=vres is not None),
        grid=(B // n_rows, S // ts),
        in_specs=in_specs,
        out_specs=tile,
        out_shape=jax.ShapeDtypeStruct((B, S, D), BF16),
        scratch_shapes=[pltpu.VMEM((n_rows * (D // L), L, L), F32)],
        compiler_params=_params(("parallel", "arbitrary")),
        name="rwkv_scan",
    )(*args).reshape(M, D)


def _rwkv_layer(x, nw, mu, w_in, w0, w1, w2, a0, a1, a2, k_k, k_a, r_k, ln_w, ln_b, w_out,
                v_first, vres):
    B, S, D = x.shape
    M = B * S
    mix = _rwkv_prep(x, nw, mu).reshape(6, M, D)
    proj = _in_proj(mix, None, w_in.astype(BF16), tn=D)
    outs = _rwkv_lora(mix, w0, w1, w2, a0, a1, a2, vres)
    if vres is None:
        (lw, a), v_gate = outs, None
        v_first = proj
    else:
        lw, a, sv = outs
        v_gate = (v_first, sv)
    yg = _rwkv_scan(proj, lw, a, v_gate, k_k, k_a, r_k, ln_w, ln_b, B, S)
    return _out_proj(yg, w_out.astype(BF16), x.reshape(M, D)).reshape(B, S, D), v_first


def _rope_body(pos_ref, invf_ref, cos_ref, sin_ref):
    ang = pos_ref[...].astype(F32) * invf_ref[...]
    cos_ref[...] = jnp.cos(ang)
    sin_ref[...] = jnp.sin(ang)


def _rope_tables(positions, ts=512):
    B, S = positions.shape
    Dh = ATT_HEAD_DIM
    ts = _tile(S, ts)
    inv_freq = ROPE_THETA ** (-jnp.arange(0, ROPE_DIM, 2, dtype=F32) / ROPE_DIM)
    invf = jnp.zeros((Dh,), F32).at[:ROPE_DIM].set(jnp.tile(inv_freq, 2)).reshape(1, Dh)
    out = pl.BlockSpec((None, ts, Dh), lambda b, s: (b, s, 0))
    shp = jax.ShapeDtypeStruct((B, S, Dh), F32)
    return pl.pallas_call(
        _rope_body,
        grid=(B, S // ts),
        in_specs=[pl.BlockSpec((None, ts, 1), lambda b, s: (b, s, 0)),
                  pl.BlockSpec((1, Dh), lambda b, s: (0, 0))],
        out_specs=[out] * 2,
        out_shape=[shp] * 2,
        compiler_params=_params(("parallel", "arbitrary")),
        name="rope_tables",
    )(positions.reshape(B, S, 1), invf)


def _attn_body(cos_ref, sin_ref, qn_ref, kn_ref, *refs):
    G = len(DILATED_CONFIG)
    q_refs, k_refs, v_refs = refs[:G], refs[G:2 * G], refs[2 * G:3 * G]
    gate_ref, out_ref, qs_ref, ks_ref = refs[3 * G:3 * G + 4]
    os_refs, ls_refs = refs[3 * G + 4:4 * G + 4], refs[4 * G + 4:]
    S, Dh = qs_ref.shape
    BLK = ATT_BLOCK
    half = ROPE_DIM // 2
    scale = Dh ** -0.5

    ri = lax.broadcasted_iota(jnp.int32, (Dh, Dh), 0)
    rj = lax.broadcasted_iota(jnp.int32, (Dh, Dh), 1)
    rot_half = (jnp.where((rj >= half) & (rj < ROPE_DIM) & (ri == rj - half), 1.0, 0.0)
                - jnp.where((rj < half) & (ri == rj + half), 1.0, 0.0)).astype(BF16)
    ones_k = jnp.ones((2 * BLK, Dh), BF16)

    def norm_rope(x, w):
        xn = _rms(x, w)
        parts = jnp.dot(jnp.concatenate(_split2(xn), axis=0), rot_half,
                        preferred_element_type=F32)
        return xn * cos_ref[...] + (parts[:S] + parts[S:]) * sin_ref[...]

    def band_mask(kw, off):
        qi = lax.broadcasted_iota(jnp.int32, (BLK, kw), 0)
        kj = lax.broadcasted_iota(jnp.int32, (BLK, kw), 1)
        return (kj >= qi + (off - BLK)) & (kj <= qi + off)

    first_mask = band_mask(BLK, 0)
    later_mask = band_mask(2 * BLK, BLK)

    for g, (_, d) in enumerate(DILATED_CONFIG):
        qs_ref[...] = norm_rope(q_refs[g][...], qn_ref[g]) * scale
        ks_ref[...] = norm_rope(k_refs[g][...], kn_ref[g])
        L = S // d
        sub = lambda rho, blk, size: pl.ds(rho + blk * BLK * d, size, stride=d)
        tiles = [(sub(rho, n, BLK),) + ((sub(rho, 0, BLK), first_mask) if n == 0 else
                                        (sub(rho, n - 1, 2 * BLK), later_mask))
                 for rho in range(d) for n in range(L // BLK)]
        for t0 in range(0, len(tiles), ATT_TILE_BATCH):
            batch = tiles[t0:t0 + ATT_TILE_BATCH]
            s = [jnp.where(mask, _dot_t(qs_ref[q_rows, :].astype(BF16),
                                        ks_ref[k_rows, :].astype(BF16)), -jnp.inf)
                 for q_rows, k_rows, mask in batch]
            m = [jnp.max(x, axis=-1, keepdims=True) for x in s]
            p = [jnp.exp(x - mx).astype(BF16) for x, mx in zip(s, m)]
            l = [jnp.dot(x, ones_k[:x.shape[1]], preferred_element_type=F32) for x in p]
            o = [jnp.dot(x, v_refs[g][k_rows, :].astype(BF16), preferred_element_type=F32)
                 for x, (_, k_rows, _) in zip(p, batch)]
            for i, (q_rows, _, _) in enumerate(batch):
                os_refs[g][q_rows, :] = o[i] / l[i]
                ls_refs[g][q_rows, :] = m[i] + jnp.log(l[i])

    def mix_rows(i, carry):
        rows = pl.ds(pl.multiple_of(i * ATT_ROWS, ATT_ROWS), ATT_ROWS)
        lse = [ls_refs[g][rows, :] for g in range(G)]
        m = functools.reduce(jnp.maximum, lse)
        e = [jnp.exp(x - m) for x in lse]
        den = functools.reduce(lambda a, b: a + b, e)
        o = functools.reduce(lambda a, b: a + b,
                             [(e[g] / den) * os_refs[g][rows, :] for g in range(G)])
        gt = gate_ref[rows, :]
        out_ref[rows, :] = (o * (gt * jax.nn.sigmoid(gt))).astype(out_ref.dtype)
        return carry

    lax.fori_loop(0, S // ATT_ROWS, mix_rows, 0, unroll=2)


def _attn_mix(proj, tables, qn_w, kn_w, B, S):
    G = len(DILATED_CONFIG)
    H = ATT_HEADS
    Dh = ATT_HEAD_DIM
    col = lambda c0: pl.BlockSpec((None, S, Dh), lambda b, h, c0=c0: (c0 + h, b, 0))
    tab = pl.BlockSpec((None, S, Dh), lambda b, h: (b, 0, 0))
    nrm = pl.BlockSpec((G, 1, Dh), lambda b, h: (0, 0, 0))
    return pl.pallas_call(
        _attn_body,
        grid=(B, H),
        in_specs=[tab, tab, nrm, nrm] + [col(c * H) for c in range(3 * G + 1)],
        out_specs=pl.BlockSpec((S, Dh), lambda b, h: (b, h)),
        out_shape=jax.ShapeDtypeStruct((B * S, H * Dh), BF16),
        scratch_shapes=[pltpu.VMEM((S, Dh), F32)] * (2 + 2 * G),
        compiler_params=_params(("parallel", "arbitrary")),
        name="dilated_attn",
    )(*tables, qn_w.reshape(G, 1, Dh), kn_w.reshape(G, 1, Dh), *([proj] * (3 * G + 1)))


def _attn_layer(x, positions, nw, w_in, qn_w, kn_w, w_out):
    B, S, D = x.shape
    M = B * S
    x2d = x.reshape(M, D)
    proj = _in_proj(x2d, nw, w_in.astype(BF16)[None], cols=True)
    og = _attn_mix(proj, _rope_tables(positions), qn_w, kn_w, B, S)
    return _out_proj(og, w_out.astype(BF16), x2d).reshape(B, S, D)


def _pool_body(u_ref, up_ref, g_ref, wg_ref, sc_ref, o_ref):
    s = pl.program_id(1)
    ts = u_ref.shape[0]
    Wg = wg_ref.shape[-1]
    t = s * ts + lax.broadcasted_iota(jnp.int32, (ts, 1), 0)
    for j, win in enumerate(POOL_WINDOWS):
        cols = slice(j * Wg, (j + 1) * Wg)
        u = u_ref[:, cols]
        halo = jnp.where(s == 0, 0.0, up_ref[:, cols])
        acc = jnp.concatenate([halo, u], axis=0)
        step = 1
        while step < win:
            acc = acc + pltpu.roll(acc, step, axis=0)
            step *= 2
        cnt = jnp.minimum(t + 1, win).astype(F32)
        diff = acc[POOL_HALO:, :] / cnt - u
        y = jnp.dot(diff.astype(BF16), wg_ref[j], preferred_element_type=F32) * sc_ref[:, cols]
        g = g_ref[:, cols]
        o_ref[:, cols] = (y * (g * jax.nn.sigmoid(g))).astype(o_ref.dtype)


def _pool_mix(proj, w_grp, scale, B, S, ts=256):
    W = proj.shape[-1] // 2
    ts = _tile(S, ts)
    hb = ts // POOL_HALO
    return pl.pallas_call(
        _pool_body,
        grid=(B, S // ts),
        in_specs=[pl.BlockSpec((None, ts, W), lambda b, s: (b, s, 0)),
                  pl.BlockSpec((None, POOL_HALO, W),
                               lambda b, s: (b, jnp.maximum(s * hb - 1, 0), 0)),
                  pl.BlockSpec((None, ts, W), lambda b, s: (b, s, 1)),
                  pl.BlockSpec(w_grp.shape, lambda b, s: (0, 0, 0)),
                  pl.BlockSpec((1, W), lambda b, s: (0, 0))],
        out_specs=pl.BlockSpec((None, ts, W), lambda b, s: (b, s, 0)),
        out_shape=jax.ShapeDtypeStruct((B, S, W), BF16),
        compiler_params=_params(("parallel", "arbitrary")),
        name="pool_mix",
    )(proj, proj, proj, w_grp.astype(BF16), scale.reshape(1, W))


def _pool_layer(x, nw, w_in, w_grp, scale, w_out):
    B, S, D = x.shape
    M = B * S
    x2d = x.reshape(M, D)
    proj = _in_proj(x2d, nw, w_in.astype(BF16)[None])[0]
    yg = _pool_mix(proj.reshape(B, S, -1), w_grp, scale, B, S)
    return _out_proj(yg.reshape(M, -1), w_out.astype(BF16), x2d).reshape(B, S, D)


def kernel(x, positions, norm_w, a_mu, a_w_in, a_w0, a_w1, a_w2, a_a0, a_a1, a_a2, a_k_k, a_k_a,
           a_r_k, a_lnx_w, a_lnx_b, a_v0, a_v1, a_v2, a_w_out, b_w_in, b_qn_w, b_kn_w, b_w_out,
           c_w_in, c_w_grp, c_scale, c_w_out):
    depth = norm_w.shape[0]
    ia = ib = ic = 0
    v_first = None
    for i in range(depth):
        kind = i % 3
        if kind == 0:
            vres = None if ia == 0 else (a_v0[ia - 1], a_v1[ia - 1], a_v2[ia - 1])
            x, v_first = _rwkv_layer(
                x, norm_w[i], a_mu[ia], a_w_in[ia], a_w0[ia], a_w1[ia], a_w2[ia], a_a0[ia],
                a_a1[ia], a_a2[ia], a_k_k[ia], a_k_a[ia], a_r_k[ia].reshape(-1), a_lnx_w[ia],
                a_lnx_b[ia], a_w_out[ia], v_first, vres)
            ia += 1
        elif kind == 1:
            x = _attn_layer(x, positions, norm_w[i], b_w_in[ib], b_qn_w[ib], b_kn_w[ib],
                            b_w_out[ib])
            ib += 1
        else:
            x = _pool_layer(x, norm_w[i], c_w_in[ic], c_w_grp[ic], c_scale[ic], c_w_out[ic])
            ic += 1
    return x
```
